```python
import functools
import jax, jax.numpy as jnp
from jax import lax
import numpy as np

D_MODEL = 2048
BATCH = 4
SEQ = 2048
DEPTH = 4
DEC_BATCH = 128
DEC_SEQ = 8
PAST_LEN = 16384
PAGE_SIZE = 128

N_META = 16
D_MIX = D_MODEL
D_CONV = D_MIX // 2
CONV_WIDTH = 3
CONV_GROUPS = 8
GLA_HEADS = 4
GLA_DV = (D_MIX - D_CONV) // GLA_HEADS
GLA_DK = GLA_DV // 2
GLA_GATE_RANK = 16
GLA_GATE_TAU = 16.0
GLA_CHUNK = 64
D_FF = 5632
EPS = 1e-6

MIX_IN_SIZES = (D_CONV, D_CONV, D_CONV, GLA_HEADS * GLA_DK, GLA_HEADS * GLA_DK,
                GLA_HEADS * GLA_DV, GLA_HEADS * GLA_DV, GLA_GATE_RANK)
D_MIX_IN = sum(MIX_IN_SIZES)
MIX_IN_OFFSETS = tuple(int(o) for o in np.cumsum(MIX_IN_SIZES)[:-1])

kernel_name = 'hybrid_conv_gla_macaron_step'


def rmsnorm(x, g):
    xf = x.astype(jnp.float32)
    y = xf * lax.rsqrt(jnp.mean(xf * xf, axis=-1, keepdims=True) + EPS)
    return (y * g.astype(jnp.float32)).astype(x.dtype)


def group_rmsnorm(x, g, groups):
    xf = x.astype(jnp.float32)
    xg = xf.reshape(x.shape[:-1] + (groups, x.shape[-1] // groups))
    yg = xg * lax.rsqrt(jnp.mean(xg * xg, axis=-1, keepdims=True) + EPS)
    return (yg.reshape(x.shape) * g.astype(jnp.float32)).astype(x.dtype)


def swiglu(x, w_gu, w_down):
    a, b = jnp.split(x @ w_gu, 2, axis=-1)
    return (jax.nn.silu(a) * b) @ w_down


def gla_chunk(S, q, k, v, g):
    L = q.shape[1]
    b = jnp.cumsum(g, axis=1)
    o_inter = jnp.einsum('blhd,bhdv->blhv', q * jnp.exp(b), S)
    causal = jnp.tril(jnp.ones((L, L), dtype=bool))[None, :, :, None, None]
    decay = jnp.exp(jnp.where(causal, b[:, :, None] - b[:, None, :], -jnp.inf))
    scores = jnp.einsum('bthd,bshd,btshd->bhts', q, k, decay)
    o_intra = jnp.einsum('bhts,bshv->bthv', scores, v)
    b_last = b[:, -1]
    S_new = jnp.exp(b_last)[..., None] * S + jnp.einsum(
        'bshd,bshv->bhdv', k * jnp.exp(b_last[:, None] - b), v)
    return S_new, o_inter + o_intra


def gla_prompt(q, k, v, g):
    Bsz, T = q.shape[:2]
    S0 = jnp.zeros((Bsz, GLA_HEADS, GLA_DK, GLA_DV), jnp.float32)
    S1, o_meta = gla_chunk(S0, q[:, :N_META], k[:, :N_META], v[:, :N_META], g[:, :N_META])
    n_chunks = (T - N_META) // GLA_CHUNK

    def to_chunks(a):
        return a[:, N_META:].reshape((Bsz, n_chunks, GLA_CHUNK) + a.shape[2:]).swapaxes(0, 1)

    S_fin, o_rest = lax.scan(lambda S, xs: gla_chunk(S, *xs), S1,
                             (to_chunks(q), to_chunks(k), to_chunks(v), to_chunks(g)))
    o_rest = o_rest.swapaxes(0, 1).reshape((Bsz, T - N_META) + o_rest.shape[3:])
    return S_fin, jnp.concatenate([o_meta, o_rest], axis=1)


def token_mixer(xn, conv_prev, gla_fn, w_in, conv_w, conv_g, fw2, fb, gla_g, w_out):
    Bsz, T, _ = xn.shape
    f32 = jnp.float32
    cB, cC, ch, q, k, v, go, fl = jnp.split(xn @ w_in, MIX_IN_OFFSETS, axis=-1)
    u = cC * ch
    pad = jnp.concatenate([conv_prev.astype(u.dtype), u], axis=1)
    conv = sum(conv_w[j] * pad[:, j:j + T] for j in range(CONV_WIDTH))
    yc = group_rmsnorm(cB * conv, conv_g, CONV_GROUPS)
    new_conv = pad[:, T:]
    q = q.astype(f32).reshape(Bsz, T, GLA_HEADS, GLA_DK) * (GLA_DK ** -0.5)
    k = k.astype(f32).reshape(Bsz, T, GLA_HEADS, GLA_DK)
    v = v.astype(f32).reshape(Bsz, T, GLA_HEADS, GLA_DV)
    glog = (jax.nn.log_sigmoid((fl @ fw2 + fb).astype(f32)) / GLA_GATE_TAU).reshape(
        Bsz, T, GLA_HEADS, GLA_DK)
    S_new, o = gla_fn(q, k, v, glog)
    o = rmsnorm(o, gla_g) * jax.nn.silu(go.astype(f32).reshape(Bsz, T, GLA_HEADS, GLA_DV))
    yg = o.reshape(Bsz, T, GLA_HEADS * GLA_DV).astype(xn.dtype)
    y = jnp.concatenate([yc, yg], axis=-1) @ w_out
    return y, new_conv, S_new


def decoder_layer(h, conv_prev, gla_fn, p):
    (n1, gu1, dn1, nm, w_in, cw, cg, fw2, fb, gg, w_out, n2, gu2, dn2) = p
    h = h + 0.5 * swiglu(rmsnorm(h, n1), gu1, dn1)
    y, new_conv, S_new = token_mixer(rmsnorm(h, nm), conv_prev, gla_fn, w_in, cw, cg, fw2, fb, gg, w_out)
    h = h + y
    h = h + 0.5 * swiglu(rmsnorm(h, n2), gu2, dn2)
    return h, new_conv, S_new


def setup_inputs(seed: int = 0) -> dict:
    key = jax.random.key(seed)
    ks = jax.random.split(key, 20)

    def nrm(k, shape, scale):
        return jax.random.normal(k, shape, jnp.float32) * scale

    def gain(k, shape):
        return 1.0 + nrm(k, shape, 0.02)

    return {
        'x_prompt': nrm(ks[0], (BATCH, SEQ, D_MODEL), 1.0),
        'x_sample': nrm(ks[1], (DEC_BATCH, DEC_SEQ, D_MODEL), 1.0),
        'state_conv': nrm(ks[2], (DEPTH, DEC_BATCH, CONV_WIDTH - 1, D_CONV), 1.0),
        'state_gla': nrm(ks[3], (DEPTH, DEC_BATCH, GLA_HEADS, GLA_DK, GLA_DV), 0.5),
        'meta_tokens': nrm(ks[4], (N_META, D_MODEL), 1.0),
        'norm_ffn1': gain(ks[5], (DEPTH, D_MODEL)),
        'w_ffn1_gu': nrm(ks[6], (DEPTH, D_MODEL, 2 * D_FF), D_MODEL ** -0.5),
        'w_ffn1_down': nrm(ks[7], (DEPTH, D_FF, D_MODEL), D_FF ** -0.5),
        'norm_mix': gain(ks[8], (DEPTH, D_MODEL)),
        'w_mix_in': nrm(ks[9], (DEPTH, D_MODEL, D_MIX_IN), D_MODEL ** -0.5),
        'conv_w': nrm(ks[10], (DEPTH, CONV_WIDTH, D_CONV), CONV_WIDTH ** -0.5),
        'conv_norm': gain(ks[11], (DEPTH, D_CONV)),
        'gla_fgate_w2': nrm(ks[12], (DEPTH, GLA_GATE_RANK, GLA_HEADS * GLA_DK), GLA_GATE_RANK ** -0.5),
        'gla_fgate_b': nrm(ks[13], (DEPTH, GLA_HEADS * GLA_DK), 0.1),
        'gla_out_norm': gain(ks[14], (DEPTH, GLA_DV)),
        'w_mix_out': nrm(ks[15], (DEPTH, D_MIX, D_MODEL), D_MIX ** -0.5),
        'norm_ffn2': gain(ks[16], (DEPTH, D_MODEL)),
        'w_ffn2_gu': nrm(ks[17], (DEPTH, D_MODEL, 2 * D_FF), D_MODEL ** -0.5),
        'w_ffn2_down': nrm(ks[18], (DEPTH, D_FF, D_MODEL), D_FF ** -0.5),
        'norm_final': gain(ks[19], (D_MODEL,)),
    }


def reference(x_prompt, x_sample, state_conv, state_gla, meta_tokens, norm_ffn1, w_ffn1_gu,
              w_ffn1_down, norm_mix, w_mix_in, conv_w, conv_norm, gla_fgate_w2, gla_fgate_b,
              gla_out_norm, w_mix_out, norm_ffn2, w_ffn2_gu, w_ffn2_down, norm_final):
    bp = x_prompt.shape[0]
    meta = jnp.broadcast_to(meta_tokens.astype(x_prompt.dtype)[None], (bp, N_META, D_MODEL))
    hp = jnp.concatenate([meta, x_prompt], axis=1)
    hs = x_sample
    conv_p, gla_p, conv_s, gla_s = [], [], [], []
    for l in range(DEPTH):
        p = (norm_ffn1[l], w_ffn1_gu[l], w_ffn1_down[l], norm_mix[l], w_mix_in[l], conv_w[l],
             conv_norm[l], gla_fgate_w2[l], gla_fgate_b[l], gla_out_norm[l], w_mix_out[l],
             norm_ffn2[l], w_ffn2_gu[l], w_ffn2_down[l])
        zeros_conv = jnp.zeros((bp, CONV_WIDTH - 1, D_CONV), hp.dtype)
        hp, c_new, S_new = decoder_layer(hp, zeros_conv, gla_prompt, p)
        conv_p.append(c_new)
        gla_p.append(S_new)
        gla_fn_s = functools.partial(gla_chunk, state_gla[l].astype(jnp.float32))
        hs, c_new, S_new = decoder_layer(hs, state_conv[l], gla_fn_s, p)
        conv_s.append(c_new)
        gla_s.append(S_new)
    y_prompt = rmsnorm(hp, norm_final)[:, N_META:]
    y_sample = rmsnorm(hs, norm_final)
    new_gla_prompt = jnp.stack(gla_p)
    new_conv_prompt = jnp.stack(conv_p)
    new_gla_sample = jnp.stack(gla_s)
    new_conv_sample = jnp.stack(conv_s)
    return (y_prompt, y_sample, new_gla_prompt, new_conv_prompt, new_gla_sample, new_conv_sample)
```

```python
import functools

import jax
import jax.numpy as jnp
from jax import lax
from jax.experimental import pallas as pl
from jax.experimental.pallas import tpu as pltpu

D_MODEL = 2048
BATCH = 4
SEQ = 2048
DEPTH = 4
DEC_BATCH = 128
DEC_SEQ = 8
N_META = 16
D_CONV = 1024
CONV_WIDTH = 3
CONV_GROUPS = 8
CONV_GROUP_WIDTH = D_CONV // CONV_GROUPS
GLA_HEADS = 4
GLA_DV = 256
GLA_DK = 128
GLA_GATE_RANK = 16
GLA_GATE_TAU = 16.0
GLA_CHUNK = 64
D_FF = 5632
EPS = 1e-6
D_QK = GLA_HEADS * GLA_DK
D_V = GLA_HEADS * GLA_DV
D_PROJ = 3 * D_CONV + 2 * D_QK + 2 * D_V

OFF_CB, OFF_CC, OFF_CH = 0, D_CONV, 2 * D_CONV
OFF_Q = 3 * D_CONV
OFF_K = OFF_Q + D_QK
OFF_V = OFF_K + D_QK
OFF_GO = OFF_V + D_V

N_PROMPT = BATCH * SEQ
N_SAMPLE = DEC_BATCH * DEC_SEQ
N_METAROWS = BATCH * N_META
ROW_SAMPLE = N_PROMPT
ROW_META = N_PROMPT + N_SAMPLE
N_TOK = ROW_META + N_METAROWS

SUBLANES = 8
LANES = 128
VMEM_LIMIT = 58 * 1024 * 1024

TM = 928
TF = 512
TN_MIX = 1024
TM_OUT = 464
SUB = 16
MIX_BLOCK = 256
SAMPLE_SEQS = 8

F32 = jnp.float32
BF16 = jnp.bfloat16


def _rms_scale(x, width):
    return lax.rsqrt(jnp.sum(x * x, axis=-1, keepdims=True) * (1.0 / width) + EPS)


def _sigmoid(x):
    return 1.0 / (1.0 + jnp.exp(-x))


def _ffn_kernel(h_ref, g_ref, wg_ref, wu_ref, wd_ref, o_ref, xn_ref):
    j = pl.program_id(1)

    @pl.when(j == 0)
    def _():
        h = h_ref[...]
        xn_ref[...] = (h * _rms_scale(h, D_MODEL) * g_ref[...]).astype(BF16)
        o_ref[...] = h

    xn = xn_ref[...]
    a = jnp.dot(xn, wg_ref[...], preferred_element_type=F32)
    b = jnp.dot(xn, wu_ref[...], preferred_element_type=F32)
    act = (a * _sigmoid(a) * (b * 0.5)).astype(BF16)
    o_ref[...] += jnp.dot(act, wd_ref[...], preferred_element_type=F32)


def _ffn(h, gain, w_gu, w_down):
    n_tok = h.shape[0]
    nf = D_FF // TF
    return pl.pallas_call(
        _ffn_kernel,
        grid=(n_tok // TM, nf),
        in_specs=[
            pl.BlockSpec((TM, D_MODEL), lambda i, j: (i, 0), pipeline_mode=pl.Buffered(1)),
            pl.BlockSpec((1, D_MODEL), lambda i, j: (0, 0)),
            pl.BlockSpec((D_MODEL, TF), lambda i, j: (0, j)),
            pl.BlockSpec((D_MODEL, TF), lambda i, j: (0, j + nf)),
            pl.BlockSpec((TF, D_MODEL), lambda i, j: (j, 0)),
        ],
        out_specs=pl.BlockSpec((TM, D_MODEL), lambda i, j: (i, 0)),
        out_shape=jax.ShapeDtypeStruct((n_tok, D_MODEL), F32),
        scratch_shapes=[pltpu.VMEM((TM, D_MODEL), BF16)],
        compiler_params=pltpu.CompilerParams(
            dimension_semantics=("arbitrary", "arbitrary"), vmem_limit_bytes=VMEM_LIMIT),
        name="ffn",
    )(h, gain, w_gu, w_gu, w_down)


def _mixin_kernel(h_ref, g_ref, w_ref, wfl_ref, fw2_ref, fb_ref, proj_ref, glog_ref, xn_ref):
    j = pl.program_id(1)

    @pl.when(j == 0)
    def _():
        h = h_ref[...]
        xn = (h * _rms_scale(h, D_MODEL) * g_ref[...]).astype(BF16)
        xn_ref[...] = xn
        fl = jnp.dot(xn, wfl_ref[...], preferred_element_type=F32)
        z = jnp.dot(fl.astype(BF16), fw2_ref[...], preferred_element_type=F32) + fb_ref[...]
        glog_ref[...] = (jnp.minimum(z, 0.0) - jnp.log1p(jnp.exp(-jnp.abs(z)))) * (1.0 / GLA_GATE_TAU)

    proj_ref[...] = jnp.dot(xn_ref[...], w_ref[...], preferred_element_type=F32)


def _mixin(h, gain, w_main, w_fl, fw2, fb):
    n_tok = h.shape[0]
    return pl.pallas_call(
        _mixin_kernel,
        grid=(n_tok // TM, D_PROJ // TN_MIX),
        in_specs=[
            pl.BlockSpec((TM, D_MODEL), lambda i, j: (i, 0), pipeline_mode=pl.Buffered(1)),
            pl.BlockSpec((1, D_MODEL), lambda i, j: (0, 0)),
            pl.BlockSpec((D_MODEL, TN_MIX), lambda i, j: (0, j)),
            pl.BlockSpec((D_MODEL, LANES), lambda i, j: (0, 0)),
            pl.BlockSpec((LANES, D_QK), lambda i, j: (0, 0)),
            pl.BlockSpec((1, D_QK), lambda i, j: (0, 0)),
        ],
        out_specs=[
            pl.BlockSpec((TM, TN_MIX), lambda i, j: (i, j)),
            pl.BlockSpec((TM, D_QK), lambda i, j: (i, 0)),
        ],
        out_shape=[
            jax.ShapeDtypeStruct((n_tok, D_PROJ), F32),
            jax.ShapeDtypeStruct((n_tok, D_QK), F32),
        ],
        scratch_shapes=[pltpu.VMEM((TM, D_MODEL), BF16)],
        compiler_params=pltpu.CompilerParams(
            dimension_semantics=("arbitrary", "arbitrary"), vmem_limit_bytes=VMEM_LIMIT),
        name="mixin",
    )(h, gain, w_main, w_fl, fw2, fb)


def _block_rows(x, block, j):
    rows, width = x.shape
    if rows == block:
        return jnp.broadcast_to(x[j:j + 1, :], (rows, width))
    x3 = x.reshape(rows // block, block, width)
    return jnp.broadcast_to(x3[:, j:j + 1, :], (rows // block, block, width)).reshape(rows, width)


def _conv_branch(cb, u, u1, u2, cw_ref, cg_ref, y_ref, rows):
    conv = cw_ref[0:1, :] * u2 + cw_ref[1:2, :] * u1 + cw_ref[2:3, :] * u
    z = cb * conv
    for grp in range(CONV_GROUPS):
        cols = slice(grp * CONV_GROUP_WIDTH, (grp + 1) * CONV_GROUP_WIDTH)
        zg = z[:, cols]
        y_ref[rows, cols] = (zg * _rms_scale(zg, CONV_GROUP_WIDTH) * cg_ref[:, cols]).astype(y_ref.dtype)


def _gla_rows(q, k, v, g, seg, sub):
    rows = q.shape[0]
    ri = lax.broadcasted_iota(jnp.int32, (rows, rows), 0)
    ci = lax.broadcasted_iota(jnp.int32, (rows, rows), 1)
    same_seg = (ri // seg) == (ci // seg) if rows != seg else None
    causal = ci <= ri
    mcum = jnp.where(causal if same_seg is None else (causal & same_seg), 1.0, 0.0).astype(F32)
    b = jnp.dot(mcum, g, preferred_element_type=F32, precision=lax.Precision.HIGHEST)
    qs = q * (GLA_DK ** -0.5)

    blk0 = (ri // sub) * sub
    dcol = jnp.where(causal & (ci >= blk0), ci - blk0, -1)
    a_heads = [jnp.zeros((rows, rows), F32) for _ in range(GLA_HEADS)]
    for j in range(sub):
        kb = _block_rows(k, sub, j)
        bb = _block_rows(b, sub, j)
        p = qs * kb * jnp.exp(jnp.minimum(b - bb, 0.0))
        hit = dcol == j
        for hd in range(GLA_HEADS):
            r = jnp.sum(p[:, hd * GLA_DK:(hd + 1) * GLA_DK], axis=-1, keepdims=True)
            a_heads[hd] = jnp.where(hit, r, a_heads[hd])

    nsub = seg // sub
    if nsub > 1:
        assert rows == seg
        off = [[jnp.zeros((sub, rows), F32)] for _ in range(GLA_HEADS)]
        ci_s = lax.broadcasted_iota(jnp.int32, (sub, rows), 1)
        for i in range(1, nsub):
            lo = i * sub
            bs = b[lo - 1:lo, :]
            qd = (qs[lo:lo + sub, :] * jnp.exp(b[lo:lo + sub, :] - bs)).astype(BF16)
            kd = (k * jnp.exp(jnp.minimum(bs - b, 0.0))).astype(BF16)
            for hd in range(GLA_HEADS):
                cols = slice(hd * GLA_DK, (hd + 1) * GLA_DK)
                a_i = lax.dot_general(qd[:, cols], kd[:, cols], (((1,), (1,)), ((), ())),
                                      preferred_element_type=F32)
                off[hd].append(jnp.where(ci_s < lo, a_i, 0.0))
        a_heads = [a_heads[hd] + jnp.concatenate(off[hd], axis=0) for hd in range(GLA_HEADS)]

    b_last = _block_rows(b, seg, seg - 1)
    qb = qs * jnp.exp(b)
    kdec = k * jnp.exp(b_last - b)
    dec_last = jnp.exp(b_last)
    return qb, kdec, dec_last, a_heads


def _gla_out(o, go, gg_ref):
    return o * _rms_scale(o, GLA_DV) * gg_ref[...] * (go * _sigmoid(go))


def _mixer_chain_kernel(proj_ref, glog_ref, cw_ref, cg_ref, gg_ref, cprev_ref, s0_ref,
                        y_ref, cnew_ref, snew_ref, s_scr, tail_scr, *, chunk, nchunks):
    blk = pl.program_id(1)
    tb = chunk * nchunks
    sub = min(SUB, chunk)

    @pl.when(blk == 0)
    def _():
        s_scr[...] = s0_ref[0]
        tail_scr[...] = cprev_ref[0]

    u = proj_ref[:, OFF_CC:OFF_CC + D_CONV] * proj_ref[:, OFF_CH:OFF_CH + D_CONV]
    prev = tail_scr[...]
    t8 = lax.broadcasted_iota(jnp.int32, (SUBLANES, D_CONV), 0)
    r1 = pltpu.roll(u, 1, 0)
    r2 = pltpu.roll(u, 2, 0)
    f1 = jnp.where(t8 < 1, pltpu.roll(prev, 1, 0), r1[0:SUBLANES])
    f2 = jnp.where(t8 < 2, pltpu.roll(prev, 2, 0), r2[0:SUBLANES])
    if tb > SUBLANES:
        u1 = jnp.concatenate([f1, r1[SUBLANES:]], axis=0)
        u2 = jnp.concatenate([f2, r2[SUBLANES:]], axis=0)
    else:
        u1, u2 = f1, f2
    tail_scr[...] = u[tb - SUBLANES:tb]
    _conv_branch(proj_ref[:, OFF_CB:OFF_CB + D_CONV], u, u1, u2, cw_ref, cg_ref, y_ref, slice(None))

    def chunk_step(c, carry):
        rows = pl.ds(pl.multiple_of(c * chunk, chunk), chunk)
        q = proj_ref[rows, OFF_Q:OFF_Q + D_QK]
        k = proj_ref[rows, OFF_K:OFF_K + D_QK]
        g = glog_ref[rows, :]
        qb, kdec, dec_last, a_heads = _gla_rows(q, k, None, g, chunk, sub)
        for hd in range(GLA_HEADS):
            kc = slice(hd * GLA_DK, (hd + 1) * GLA_DK)
            v = proj_ref[rows, OFF_V + hd * GLA_DV:OFF_V + (hd + 1) * GLA_DV].astype(BF16)
            go = proj_ref[rows, OFF_GO + hd * GLA_DV:OFF_GO + (hd + 1) * GLA_DV]
            s = s_scr[hd]
            o = jnp.dot(qb[:, kc].astype(BF16), s.astype(BF16), preferred_element_type=F32)
            o = o + jnp.dot(a_heads[hd].astype(BF16), v, preferred_element_type=F32)
            y_ref[rows, D_CONV + hd * GLA_DV:D_CONV + (hd + 1) * GLA_DV] = (
                _gla_out(o, go, gg_ref).astype(y_ref.dtype))
            kdec_t = kdec[:, kc].T.astype(BF16)
            dec_col = jnp.broadcast_to(dec_last[0:1, kc], (SUBLANES, GLA_DK)).T[:, 0:1]
            s_scr[hd] = s * dec_col + jnp.dot(kdec_t, v, preferred_element_type=F32)
        return carry

    lax.fori_loop(0, nchunks, chunk_step, 0)

    @pl.when(blk == pl.num_programs(1) - 1)
    def _():
        snew_ref[0] = s_scr[...]
        cnew_ref[0] = tail_scr[...]


def _mixer_chain(proj, glog, cw, cg, gg, cprev, s0, *, row0, n_seq, seq_len, chunk, block):
    nblk = seq_len // block
    base = row0 // block
    assert row0 % block == 0 and seq_len % block == 0 and block % chunk == 0
    kern = functools.partial(_mixer_chain_kernel, chunk=chunk, nchunks=block // chunk)
    return pl.pallas_call(
        kern,
        grid=(n_seq, nblk),
        in_specs=[
            pl.BlockSpec((block, D_PROJ), lambda s, i: (base + s * nblk + i, 0)),
            pl.BlockSpec((block, D_QK), lambda s, i: (base + s * nblk + i, 0)),
            pl.BlockSpec((CONV_WIDTH, D_CONV), lambda s, i: (0, 0)),
            pl.BlockSpec((1, D_CONV), lambda s, i: (0, 0)),
            pl.BlockSpec((1, GLA_DV), lambda s, i: (0, 0)),
            pl.BlockSpec((1, SUBLANES, D_CONV), lambda s, i: (s, 0, 0)),
            pl.BlockSpec((1, GLA_HEADS, GLA_DK, GLA_DV), lambda s, i: (s, 0, 0, 0)),
        ],
        out_specs=[
            pl.BlockSpec((block, D_MODEL), lambda s, i: (s * nblk + i, 0)),
            pl.BlockSpec((1, SUBLANES, D_CONV), lambda s, i: (s, 0, 0)),
            pl.BlockSpec((1, GLA_HEADS, GLA_DK, GLA_DV), lambda s, i: (s, 0, 0, 0)),
        ],
        out_shape=[
            jax.ShapeDtypeStruct((n_seq * seq_len, D_MODEL), BF16),
            jax.ShapeDtypeStruct((n_seq, SUBLANES, D_CONV), F32),
            jax.ShapeDtypeStruct((n_seq, GLA_HEADS, GLA_DK, GLA_DV), F32),
        ],
        scratch_shapes=[
            pltpu.VMEM((GLA_HEADS, GLA_DK, GLA_DV), F32),
            pltpu.VMEM((SUBLANES, D_CONV), F32),
        ],
        compiler_params=pltpu.CompilerParams(
            dimension_semantics=("arbitrary", "arbitrary"), vmem_limit_bytes=VMEM_LIMIT),
        name=f"mixer_chain{chunk}",
    )(proj, glog, cw, cg, gg, cprev, s0)


def _mixer_step_kernel(proj_ref, glog_ref, cw_ref, cg_ref, gg_ref, cprev_ref, s0_ref,
                       y_ref, cnew_ref, snew_ref, *, nseq, seq_len):
    rows = nseq * seq_len
    assert seq_len == SUBLANES

    u = proj_ref[:, OFF_CC:OFF_CC + D_CONV] * proj_ref[:, OFF_CH:OFF_CH + D_CONV]
    u3 = u.reshape(nseq, seq_len, D_CONV)
    prev = cprev_ref[...]
    t8 = lax.broadcasted_iota(jnp.int32, (nseq, seq_len, D_CONV), 1)
    u1 = jnp.where(t8 < 1, pltpu.roll(prev, 1, 1), pltpu.roll(u3, 1, 1)).reshape(rows, D_CONV)
    u2 = jnp.where(t8 < 2, pltpu.roll(prev, 2, 1), pltpu.roll(u3, 2, 1)).reshape(rows, D_CONV)
    cnew_ref[...] = u3
    _conv_branch(proj_ref[:, OFF_CB:OFF_CB + D_CONV], u, u1, u2, cw_ref, cg_ref, y_ref, slice(None))

    q = proj_ref[:, OFF_Q:OFF_Q + D_QK]
    k = proj_ref[:, OFF_K:OFF_K + D_QK]
    g = glog_ref[...]
    qb, kdec, dec_last, a_heads = _gla_rows(q, k, None, g, seq_len, seq_len)

    qmask = (lax.broadcasted_iota(jnp.int32, (rows, nseq * GLA_DK), 0) // seq_len
             == lax.broadcasted_iota(jnp.int32, (rows, nseq * GLA_DK), 1) // GLA_DK)
    kmask = (lax.broadcasted_iota(jnp.int32, (nseq * GLA_DK, rows), 0) // GLA_DK
             == lax.broadcasted_iota(jnp.int32, (nseq * GLA_DK, rows), 1) // seq_len)
    for hd in range(GLA_HEADS):
        kc = slice(hd * GLA_DK, (hd + 1) * GLA_DK)
        v = proj_ref[:, OFF_V + hd * GLA_DV:OFF_V + (hd + 1) * GLA_DV].astype(BF16)
        go = proj_ref[:, OFF_GO + hd * GLA_DV:OFF_GO + (hd + 1) * GLA_DV]
        s = s0_ref[:, hd].reshape(nseq * GLA_DK, GLA_DV)
        q_blk = jnp.where(qmask, jnp.tile(qb[:, kc], (1, nseq)), 0.0).astype(BF16)
        o = jnp.dot(q_blk, s.astype(BF16), preferred_element_type=F32)
        o = o + jnp.dot(a_heads[hd].astype(BF16), v, preferred_element_type=F32)
        y_ref[:, D_CONV + hd * GLA_DV:D_CONV + (hd + 1) * GLA_DV] = (
            _gla_out(o, go, gg_ref).astype(y_ref.dtype))
        kdec_t = kdec[:, kc].T
        k_blk = jnp.where(kmask, jnp.tile(kdec_t, (nseq, 1)), 0.0).astype(BF16)
        upd = jnp.dot(k_blk, v, preferred_element_type=F32)
        dec_t = dec_last[:, kc].T
        for n in range(nseq):
            dec_col = dec_t[:, n * seq_len:n * seq_len + 1]
            snew_ref[n, hd] = (s[n * GLA_DK:(n + 1) * GLA_DK, :] * dec_col
                               + upd[n * GLA_DK:(n + 1) * GLA_DK, :])


def _mixer_step(proj, glog, cw, cg, gg, cprev, state_gla, layer, *, row0, n_seq, seq_len):
    rows = SAMPLE_SEQS * seq_len
    base = row0 // rows
    assert row0 % rows == 0 and n_seq % SAMPLE_SEQS == 0
    kern = functools.partial(_mixer_step_kernel, nseq=SAMPLE_SEQS, seq_len=seq_len)
    return pl.pallas_call(
        kern,
        grid=(n_seq // SAMPLE_SEQS,),
        in_specs=[
            pl.BlockSpec((rows, D_PROJ), lambda i: (base + i, 0)),
            pl.BlockSpec((rows, D_QK), lambda i: (base + i, 0)),
            pl.BlockSpec((CONV_WIDTH, D_CONV), lambda i: (0, 0)),
            pl.BlockSpec((1, D_CONV), lambda i: (0, 0)),
            pl.BlockSpec((1, GLA_DV), lambda i: (0, 0)),
            pl.BlockSpec((SAMPLE_SEQS, SUBLANES, D_CONV), lambda i: (i, 0, 0)),
            pl.BlockSpec((None, SAMPLE_SEQS, GLA_HEADS, GLA_DK, GLA_DV), lambda i: (layer, i, 0, 0, 0)),
        ],
        out_specs=[
            pl.BlockSpec((rows, D_MODEL), lambda i: (i, 0)),
            pl.BlockSpec((SAMPLE_SEQS, SUBLANES, D_CONV), lambda i: (i, 0, 0)),
            pl.BlockSpec((SAMPLE_SEQS, GLA_HEADS, GLA_DK, GLA_DV), lambda i: (i, 0, 0, 0)),
        ],
        out_shape=[
            jax.ShapeDtypeStruct((n_seq * seq_len, D_MODEL), BF16),
            jax.ShapeDtypeStruct((n_seq, SUBLANES, D_CONV), F32),
            jax.ShapeDtypeStruct((n_seq, GLA_HEADS, GLA_DK, GLA_DV), F32),
        ],
        compiler_params=pltpu.CompilerParams(
            dimension_semantics=("arbitrary",), vmem_limit_bytes=VMEM_LIMIT),
        name="mixer_step",
    )(proj, glog, cw, cg, gg, cprev, state_gla)


def _mixout_kernel(h_ref, y_ref, w_ref, o_ref):
    o_ref[...] = h_ref[...] + jnp.dot(y_ref[...], w_ref[...], preferred_element_type=F32)


def _mixout(h, y, w_out):
    n_tok = h.shape[0]
    return pl.pallas_call(
        _mixout_kernel,
        grid=(n_tok // TM_OUT,),
        in_specs=[
            pl.BlockSpec((TM_OUT, D_MODEL), lambda i: (i, 0)),
            pl.BlockSpec((TM_OUT, D_MODEL), lambda i: (i, 0)),
            pl.BlockSpec((D_MODEL, D_MODEL), lambda i: (0, 0)),
        ],
        out_specs=pl.BlockSpec((TM_OUT, D_MODEL), lambda i: (i, 0)),
        out_shape=jax.ShapeDtypeStruct((n_tok, D_MODEL), F32),
        compiler_params=pltpu.CompilerParams(
            dimension_semantics=("arbitrary",), vmem_limit_bytes=VMEM_LIMIT),
        name="mixout",
    )(h, y, w_out)


def _final_norm_kernel(h_ref, g_ref, o_ref):
    h = h_ref[...]
    o_ref[...] = h * _rms_scale(h, D_MODEL) * g_ref[...]


def _final_norm(h, gain, *, row0, n_rows, tile):
    base = row0 // tile
    assert row0 % tile == 0 and n_rows % tile == 0
    return pl.pallas_call(
        _final_norm_kernel,
        grid=(n_rows // tile,),
        in_specs=[
            pl.BlockSpec((tile, D_MODEL), lambda i: (base + i, 0)),
            pl.BlockSpec((1, D_MODEL), lambda i: (0, 0)),
        ],
        out_specs=pl.BlockSpec((tile, D_MODEL), lambda i: (i, 0)),
        out_shape=jax.ShapeDtypeStruct((n_rows, D_MODEL), F32),
        compiler_params=pltpu.CompilerParams(
            dimension_semantics=("arbitrary",), vmem_limit_bytes=VMEM_LIMIT),
        name="final_norm",
    )(h, gain)


def _pad_conv_state(c):
    return jnp.pad(c, ((0, 0), (SUBLANES - (CONV_WIDTH - 1), 0), (0, 0)))


def kernel(x_prompt, x_sample, state_conv, state_gla, meta_tokens, norm_ffn1, w_ffn1_gu, w_ffn1_down,
           norm_mix, w_mix_in, conv_w, conv_norm, gla_fgate_w2, gla_fgate_b, gla_out_norm, w_mix_out,
           norm_ffn2, w_ffn2_gu, w_ffn2_down, norm_final):
    assert x_prompt.shape == (BATCH, SEQ, D_MODEL) and x_sample.shape == (DEC_BATCH, DEC_SEQ, D_MODEL)
    h = jnp.concatenate([
        x_prompt.reshape(N_PROMPT, D_MODEL),
        x_sample.reshape(N_SAMPLE, D_MODEL),
        jnp.tile(meta_tokens.astype(x_prompt.dtype), (BATCH, 1)),
    ], axis=0)

    wgu1 = w_ffn1_gu.astype(BF16)
    wd1 = w_ffn1_down.astype(BF16)
    wgu2 = w_ffn2_gu.astype(BF16)
    wd2 = w_ffn2_down.astype(BF16)
    w_in = w_mix_in[:, :, :D_PROJ].astype(BF16)
    w_fl = jnp.pad(w_mix_in[:, :, D_PROJ:], ((0, 0), (0, 0), (0, LANES - GLA_GATE_RANK))).astype(BF16)
    fw2 = jnp.pad(gla_fgate_w2, ((0, 0), (0, LANES - GLA_GATE_RANK), (0, 0))).astype(BF16)
    w_out = w_mix_out.astype(BF16)

    zero_conv = jnp.zeros((BATCH, SUBLANES, D_CONV), F32)
    zero_gla = jnp.zeros((BATCH, GLA_HEADS, GLA_DK, GLA_DV), F32)
    conv_p, gla_p, conv_s, gla_s = [], [], [], []
    for l in range(DEPTH):
        h = _ffn(h, norm_ffn1[l][None], wgu1[l], wd1[l])
        proj, glog = _mixin(h, norm_mix[l][None], w_in[l], w_fl[l], fw2[l], gla_fgate_b[l][None])
        mix_w = (conv_w[l], conv_norm[l][None], gla_out_norm[l][None])
        y_m, c_m, s_m = _mixer_chain(proj, glog, *mix_w, zero_conv, zero_gla, row0=ROW_META,
                                     n_seq=BATCH, seq_len=N_META, chunk=N_META, block=N_META)
        y_p, c_p, s_p = _mixer_chain(proj, glog, *mix_w, c_m, s_m, row0=0,
                                     n_seq=BATCH, seq_len=SEQ, chunk=GLA_CHUNK, block=MIX_BLOCK)
        y_s, c_s, s_s = _mixer_step(proj, glog, *mix_w, _pad_conv_state(state_conv[l]), state_gla, l,
                                    row0=ROW_SAMPLE, n_seq=DEC_BATCH, seq_len=DEC_SEQ)
        h = _mixout(h, jnp.concatenate([y_p, y_s, y_m], axis=0), w_out[l])
        h = _ffn(h, norm_ffn2[l][None], wgu2[l], wd2[l])
        conv_p.append(c_p[:, SUBLANES - (CONV_WIDTH - 1):])
        gla_p.append(s_p)
        conv_s.append(c_s[:, SUBLANES - (CONV_WIDTH - 1):])
        gla_s.append(s_s)

    y_prompt = _final_norm(h, norm_final[None], row0=0, n_rows=N_PROMPT, tile=512)
    y_sample = _final_norm(h, norm_final[None], row0=ROW_SAMPLE, n_rows=N_SAMPLE, tile=512)
    return (y_prompt.reshape(BATCH, SEQ, D_MODEL), y_sample.reshape(DEC_BATCH, DEC_SEQ, D_MODEL),
            jnp.stack(gla_p), jnp.stack(conv_p), jnp.stack(gla_s), jnp.stack(conv_s))
```

```python
import functools

import jax
import jax.numpy as jnp
from jax import lax
from jax.experimental import pallas as pl
from jax.experimental.pallas import tpu as pltpu

D_MODEL = 2048
BATCH = 4
SEQ = 2048
DEPTH = 4
DEC_BATCH = 128
DEC_SEQ = 8
N_META = 16
D_CONV = 1024
CONV_WIDTH = 3
CONV_GROUPS = 8
CONV_GROUP_WIDTH = D_CONV // CONV_GROUPS
GLA_HEADS = 4
GLA_DV = 256
GLA_DK = 128
GLA_GATE_RANK = 16
GLA_GATE_TAU = 16.0
GLA_CHUNK = 64
D_FF = 5632
EPS = 1e-6
D_QK = GLA_HEADS * GLA_DK
D_V = GLA_HEADS * GLA_DV
D_PROJ = 3 * D_CONV + 2 * D_QK + 2 * D_V

OFF_CB, OFF_CC, OFF_CH = 0, D_CONV, 2 * D_CONV
OFF_Q = 3 * D_CONV
OFF_K = OFF_Q + D_QK
OFF_V = OFF_K + D_QK
OFF_GO = OFF_V + D_V

N_PROMPT = BATCH * SEQ
N_SAMPLE = DEC_BATCH * DEC_SEQ
N_METAROWS = BATCH * N_META
ROW_SAMPLE = N_PROMPT
ROW_META = N_PROMPT + N_SAMPLE
N_TOK = ROW_META + N_METAROWS

SUBLANES = 8
LANES = 128
VMEM_LIMIT = 58 * 1024 * 1024

TM = 928
TF = 512
TN_MIX = 2048
TM_OUT = 464
SUB = 16
MAX_FACTORED_DECAY = 40.0
MIX_BLOCK = 256
SAMPLE_SEQS = 8

F32 = jnp.float32
BF16 = jnp.bfloat16


def _rms_scale(x, width):
    return lax.rsqrt(jnp.sum(x * x, axis=-1, keepdims=True) * (1.0 / width) + EPS)


def _sigmoid(x):
    return 1.0 / (1.0 + jnp.exp(-x))


def _ffn_kernel(h_ref, g_ref, wg_ref, wu_ref, wd_ref, o_ref, xn_ref):
    j = pl.program_id(1)

    @pl.when(j == 0)
    def _():
        h = h_ref[...]
        xn_ref[...] = (h * _rms_scale(h, D_MODEL) * g_ref[...]).astype(BF16)
        o_ref[...] = h

    xn = xn_ref[...]
    a = jnp.dot(xn, wg_ref[...].astype(BF16), preferred_element_type=F32)
    b = jnp.dot(xn, wu_ref[...].astype(BF16), preferred_element_type=F32)
    act = (a * _sigmoid(a) * (b * 0.5)).astype(BF16)
    o_ref[...] += jnp.dot(act, wd_ref[...].astype(BF16), preferred_element_type=F32)


def _ffn(h, gain, w_gu, w_down, layer):
    n_tok = h.shape[0]
    nf = D_FF // TF
    return pl.pallas_call(
        _ffn_kernel,
        grid=(n_tok // TM, nf),
        in_specs=[
            pl.BlockSpec((TM, D_MODEL), lambda i, j: (i, 0), pipeline_mode=pl.Buffered(1)),
            pl.BlockSpec((None, 1, D_MODEL), lambda i, j: (layer, 0, 0)),
            pl.BlockSpec((None, D_MODEL, TF), lambda i, j: (layer, 0, j)),
            pl.BlockSpec((None, D_MODEL, TF), lambda i, j: (layer, 0, j + nf)),
            pl.BlockSpec((None, TF, D_MODEL), lambda i, j: (layer, j, 0)),
        ],
        out_specs=pl.BlockSpec((TM, D_MODEL), lambda i, j: (i, 0)),
        out_shape=jax.ShapeDtypeStruct((n_tok, D_MODEL), F32),
        scratch_shapes=[pltpu.VMEM((TM, D_MODEL), BF16)],
        compiler_params=pltpu.CompilerParams(
            dimension_semantics=("arbitrary", "arbitrary"), vmem_limit_bytes=VMEM_LIMIT),
        name="ffn",
    )(h, gain, w_gu, w_gu, w_down)


def _mixin_kernel(h_ref, g_ref, w_ref, wfl_ref, fw2_ref, fb_ref, proj_ref, glog_ref, xn_ref):
    j = pl.program_id(1)

    @pl.when(j == 0)
    def _():
        h = h_ref[...]
        xn = (h * _rms_scale(h, D_MODEL) * g_ref[...]).astype(BF16)
        xn_ref[...] = xn
        fl = jnp.dot(xn, wfl_ref[...], preferred_element_type=F32)
        fl = jnp.where(lax.broadcasted_iota(jnp.int32, fl.shape, 1) < GLA_GATE_RANK, fl, 0.0)
        z =jnp.dot(fl.astype(BF16), fw2_ref[...], preferred_element_type=F32) + fb_ref[...]
        glog_ref[...] = (jnp.minimum(z, 0.0) - jnp.log1p(jnp.exp(-jnp.abs(z)))) * (1.0 / GLA_GATE_TAU)

    proj_ref[...] = jnp.dot(xn_ref[...], w_ref[...], preferred_element_type=F32)


def _mixin(h, gain, w_in, fw2, fb, layer):
    n_tok = h.shape[0]
    return pl.pallas_call(
        _mixin_kernel,
        grid=(n_tok // TM, D_PROJ // TN_MIX),
        in_specs=[
            pl.BlockSpec((TM, D_MODEL), lambda i, j: (i, 0), pipeline_mode=pl.Buffered(1)),
            pl.BlockSpec((None, 1, D_MODEL), lambda i, j: (layer, 0, 0)),
            pl.BlockSpec((None, D_MODEL, TN_MIX), lambda i, j: (layer, 0, j)),
            pl.BlockSpec((None, D_MODEL, LANES), lambda i, j: (layer, 0, D_PROJ // LANES)),
            pl.BlockSpec((None, LANES, D_QK), lambda i, j: (layer, 0, 0)),
            pl.BlockSpec((None, 1, D_QK), lambda i, j: (layer, 0, 0)),
        ],
        out_specs=[
            pl.BlockSpec((TM, TN_MIX), lambda i, j: (i, j)),
            pl.BlockSpec((TM, D_QK), lambda i, j: (i, 0)),
        ],
        out_shape=[
            jax.ShapeDtypeStruct((n_tok, D_PROJ), F32),
            jax.ShapeDtypeStruct((n_tok, D_QK), F32),
        ],
        scratch_shapes=[pltpu.VMEM((TM, D_MODEL), BF16)],
        compiler_params=pltpu.CompilerParams(
            dimension_semantics=("arbitrary", "arbitrary"), vmem_limit_bytes=VMEM_LIMIT),
        name="mixin",
    )(h, gain, w_in, w_in, fw2, fb)


def _block_rows(x, block, j):
    rows, width = x.shape
    if rows == block:
        return jnp.broadcast_to(x[j:j + 1, :], (rows, width))
    x3 = x.reshape(rows // block, block, width)
    return jnp.broadcast_to(x3[:, j:j + 1, :], (rows // block, block, width)).reshape(rows, width)


def _conv_branch(cb, u, u1, u2, cw_ref, cg_ref, y_ref, rows):
    conv = cw_ref[0:1, :] * u2 + cw_ref[1:2, :] * u1 + cw_ref[2:3, :] * u
    z = cb * conv
    for grp in range(CONV_GROUPS):
        cols = slice(grp * CONV_GROUP_WIDTH, (grp + 1) * CONV_GROUP_WIDTH)
        zg = z[:, cols]
        y_ref[rows, cols] = (zg * _rms_scale(zg, CONV_GROUP_WIDTH) * cg_ref[:, cols]).astype(y_ref.dtype)


def _gla_rows(q, k, g, seg, sub, single_ref):
    rows = q.shape[0]
    ri = lax.broadcasted_iota(jnp.int32, (rows, rows), 0)
    ci = lax.broadcasted_iota(jnp.int32, (rows, rows), 1)
    causal = ci <= ri
    if rows != seg:
        causal = causal & ((ri // seg) == (ci // seg))
    mcum = jnp.where(causal, 1.0, 0.0).astype(F32)
    b = jnp.dot(mcum, g, preferred_element_type=F32, precision=lax.Precision.HIGHEST)
    qs = q * (GLA_DK ** -0.5)
    b_last = _block_rows(b, seg, seg - 1)
    qb = qs * jnp.exp(b)
    dec_last = jnp.exp(b_last)

    if single_ref:
        kinv = k * jnp.exp(-b)
        kdec = kinv * dec_last
        qb16, kinv16 = qb.astype(BF16), kinv.astype(BF16)
        a_heads = []
        for hd in range(GLA_HEADS):
            cols = slice(hd * GLA_DK, (hd + 1) * GLA_DK)
            a_h = lax.dot_general(qb16[:, cols], kinv16[:, cols], (((1,), (1,)), ((), ())),
                                  preferred_element_type=F32)
            a_heads.append(jnp.where(causal, a_h, 0.0))
        return qb, kdec, dec_last, a_heads

    blk0 = (ri // sub) * sub
    dcol = jnp.where(causal & (ci >= blk0), ci - blk0, -1)
    a_heads = [jnp.zeros((rows, rows), F32) for _ in range(GLA_HEADS)]
    for j in range(sub):
        kb = _block_rows(k, sub, j)
        bb = _block_rows(b, sub, j)
        p = qs * kb * jnp.exp(jnp.minimum(b - bb, 0.0))
        hit = dcol == j
        for hd in range(GLA_HEADS):
            r = jnp.sum(p[:, hd * GLA_DK:(hd + 1) * GLA_DK], axis=-1, keepdims=True)
            a_heads[hd] = jnp.where(hit, r, a_heads[hd])

    nsub = seg // sub
    if nsub > 1:
        assert rows == seg
        off = [[jnp.zeros((sub, rows), F32)] for _ in range(GLA_HEADS)]
        ci_s = lax.broadcasted_iota(jnp.int32, (sub, rows), 1)
        for i in range(1, nsub):
            lo = i * sub
            bs = b[lo - 1:lo, :]
            qd = (qs[lo:lo + sub, :] * jnp.exp(b[lo:lo + sub, :] - bs)).astype(BF16)
            kd = (k * jnp.exp(jnp.minimum(bs - b, 0.0))).astype(BF16)
            for hd in range(GLA_HEADS):
                cols = slice(hd * GLA_DK, (hd + 1) * GLA_DK)
                a_i = lax.dot_general(qd[:, cols], kd[:, cols], (((1,), (1,)), ((), ())),
                                      preferred_element_type=F32)
                off[hd].append(jnp.where(ci_s < lo, a_i, 0.0))
        a_heads = [a_heads[hd] + jnp.concatenate(off[hd], axis=0) for hd in range(GLA_HEADS)]

    kdec = k * jnp.exp(b_last - b)
    return qb, kdec, dec_last, a_heads


def _gla_out(o, go, gg_ref):
    return o * _rms_scale(o, GLA_DV) * gg_ref[...] * (go * _sigmoid(go))


def _mixer_chain_kernel(proj_ref, glog_ref, cw_ref, cg_ref, gg_ref, cprev_ref, s0_ref, y_in_ref,
                        y_ref, cnew_ref, snew_ref, s_scr, tail_scr, *, chunk, nchunks):
    del y_in_ref
    blk = pl.program_id(1)
    tb = chunk * nchunks
    sub = min(SUB, chunk)

    @pl.when(blk == 0)
    def _():
        s_scr[...] = s0_ref[0]
        tail_scr[...] = cprev_ref[0]

    u = proj_ref[:, OFF_CC:OFF_CC + D_CONV] * proj_ref[:, OFF_CH:OFF_CH + D_CONV]
    prev = tail_scr[...]
    t8 = lax.broadcasted_iota(jnp.int32, (SUBLANES, D_CONV), 0)
    r1 = pltpu.roll(u, 1, 0)
    r2 = pltpu.roll(u, 2, 0)
    f1 = jnp.where(t8 < 1, pltpu.roll(prev, 1, 0), r1[0:SUBLANES])
    f2 = jnp.where(t8 < 2, pltpu.roll(prev, 2, 0), r2[0:SUBLANES])
    if tb > SUBLANES:
        u1 = jnp.concatenate([f1, r1[SUBLANES:]], axis=0)
        u2 = jnp.concatenate([f2, r2[SUBLANES:]], axis=0)
    else:
        u1, u2 = f1, f2
    tail_scr[...] = u[tb - SUBLANES:tb]
    _conv_branch(proj_ref[:, OFF_CB:OFF_CB + D_CONV], u, u1, u2, cw_ref, cg_ref, y_ref, slice(None))

    def chunk_step(single_ref, c, carry):
        rows = pl.ds(pl.multiple_of(c * chunk, chunk), chunk)
        q = proj_ref[rows, OFF_Q:OFF_Q + D_QK]
        k = proj_ref[rows, OFF_K:OFF_K + D_QK]
        g = glog_ref[rows, :]
        qb, kdec, dec_last, a_heads = _gla_rows(q, k, g, chunk, sub, single_ref)
        for hd in range(GLA_HEADS):
            kc = slice(hd * GLA_DK, (hd + 1) * GLA_DK)
            v = proj_ref[rows, OFF_V + hd * GLA_DV:OFF_V + (hd + 1) * GLA_DV].astype(BF16)
            go = proj_ref[rows, OFF_GO + hd * GLA_DV:OFF_GO + (hd + 1) * GLA_DV]
            s = s_scr[hd]
            o = jnp.dot(qb[:, kc].astype(BF16), s.astype(BF16), preferred_element_type=F32)
            o = o + jnp.dot(a_heads[hd].astype(BF16), v, preferred_element_type=F32)
            y_ref[rows, D_CONV + hd * GLA_DV:D_CONV + (hd + 1) * GLA_DV] = (
                _gla_out(o, go, gg_ref).astype(y_ref.dtype))
            kdec_t = kdec[:, kc].T.astype(BF16)
            dec_col = jnp.broadcast_to(dec_last[0:1, kc], (SUBLANES, GLA_DK)).T[:, 0:1]
            s_scr[hd] = s * dec_col + jnp.dot(kdec_t, v, preferred_element_type=F32)
        return carry

    if sub == chunk:
        lax.fori_loop(0, nchunks, functools.partial(chunk_step, False), 0)
    else:
        decay_bound = -chunk * jnp.min(glog_ref[...])
        lax.cond(decay_bound <= MAX_FACTORED_DECAY,
                 lambda: lax.fori_loop(0, nchunks, functools.partial(chunk_step, True), 0),
                 lambda: lax.fori_loop(0, nchunks, functools.partial(chunk_step, False), 0))

    @pl.when(blk == pl.num_programs(1) - 1)
    def _():
        snew_ref[0] = s_scr[...]
        cnew_ref[0] = tail_scr[...]


def _mixer_chain(proj, glog, cw, cg, gg, cprev, s0, y_buf, layer, *, row0, n_seq, seq_len, chunk, block):
    nblk = seq_len // block
    base = row0 // block
    assert row0 % block == 0 and seq_len % block == 0 and block % chunk == 0
    kern = functools.partial(_mixer_chain_kernel, chunk=chunk, nchunks=block // chunk)
    return pl.pallas_call(
        kern,
        grid=(n_seq, nblk),
        in_specs=[
            pl.BlockSpec((block, D_PROJ), lambda s, i: (base + s * nblk + i, 0)),
            pl.BlockSpec((block, D_QK), lambda s, i: (base + s * nblk + i, 0)),
            pl.BlockSpec((None, CONV_WIDTH, D_CONV), lambda s, i: (layer, 0, 0)),
            pl.BlockSpec((None, 1, D_CONV), lambda s, i: (layer, 0, 0)),
            pl.BlockSpec((None, 1, GLA_DV), lambda s, i: (layer, 0, 0)),
            pl.BlockSpec((1, SUBLANES, D_CONV), lambda s, i: (s, 0, 0)),
            pl.BlockSpec((1, GLA_HEADS, GLA_DK, GLA_DV), lambda s, i: (s, 0, 0, 0)),
            pl.BlockSpec(memory_space=pl.ANY),
        ],
        out_specs=[
            pl.BlockSpec((block, D_MODEL), lambda s, i: (base + s * nblk + i, 0)),
            pl.BlockSpec((1, SUBLANES, D_CONV), lambda s, i: (s, 0, 0)),
            pl.BlockSpec((1, GLA_HEADS, GLA_DK, GLA_DV), lambda s, i: (s, 0, 0, 0)),
        ],
        out_shape=[
            jax.ShapeDtypeStruct(y_buf.shape, y_buf.dtype),
            jax.ShapeDtypeStruct((n_seq, SUBLANES, D_CONV), F32),
            jax.ShapeDtypeStruct((n_seq, GLA_HEADS, GLA_DK, GLA_DV), F32),
        ],
        scratch_shapes=[
            pltpu.VMEM((GLA_HEADS, GLA_DK, GLA_DV), F32),
            pltpu.VMEM((SUBLANES, D_CONV), F32),
        ],
        input_output_aliases={7: 0},
        compiler_params=pltpu.CompilerParams(
            dimension_semantics=("arbitrary", "arbitrary"), vmem_limit_bytes=VMEM_LIMIT),
        name=f"mixer_chain{chunk}",
    )(proj, glog, cw, cg, gg, cprev, s0, y_buf)


def _mixer_step_kernel(proj_ref, glog_ref, cw_ref, cg_ref, gg_ref, cprev_ref, s0_ref,
                       y_in_ref, cnew_in_ref, snew_in_ref, y_ref, cnew_ref, snew_ref, *, nseq, seq_len):
    del y_in_ref, cnew_in_ref, snew_in_ref
    rows = nseq * seq_len
    assert seq_len == SUBLANES

    u = proj_ref[:, OFF_CC:OFF_CC + D_CONV] * proj_ref[:, OFF_CH:OFF_CH + D_CONV]
    u3 = u.reshape(nseq, seq_len, D_CONV)
    prev = cprev_ref[...]
    t8 = lax.broadcasted_iota(jnp.int32, (nseq, seq_len, D_CONV), 1)
    u1 = jnp.where(t8 < 1, pltpu.roll(prev, 1, 1), pltpu.roll(u3, 1, 1)).reshape(rows, D_CONV)
    u2 = jnp.where(t8 < 2, pltpu.roll(prev, 2, 1), pltpu.roll(u3, 2, 1)).reshape(rows, D_CONV)
    cnew_ref[...] = u3
    _conv_branch(proj_ref[:, OFF_CB:OFF_CB + D_CONV], u, u1, u2, cw_ref, cg_ref, y_ref, slice(None))

    q = proj_ref[:, OFF_Q:OFF_Q + D_QK]
    k = proj_ref[:, OFF_K:OFF_K + D_QK]
    g = glog_ref[...]
    qb, kdec, dec_last, a_heads = _gla_rows(q, k, g, seq_len, seq_len, False)

    qmask = (lax.broadcasted_iota(jnp.int32, (rows, nseq * GLA_DK), 0) // seq_len
             == lax.broadcasted_iota(jnp.int32, (rows, nseq * GLA_DK), 1) // GLA_DK)
    kmask = (lax.broadcasted_iota(jnp.int32, (nseq * GLA_DK, rows), 0) // GLA_DK
             == lax.broadcasted_iota(jnp.int32, (nseq * GLA_DK, rows), 1) // seq_len)
    for hd in range(GLA_HEADS):
        kc = slice(hd * GLA_DK, (hd + 1) * GLA_DK)
        v = proj_ref[:, OFF_V + hd * GLA_DV:OFF_V + (hd + 1) * GLA_DV].astype(BF16)
        go = proj_ref[:, OFF_GO + hd * GLA_DV:OFF_GO + (hd + 1) * GLA_DV]
        s = s0_ref[:, hd].reshape(nseq * GLA_DK, GLA_DV)
        q_blk = jnp.where(qmask, jnp.tile(qb[:, kc], (1, nseq)), 0.0).astype(BF16)
        o = jnp.dot(q_blk, s.astype(BF16), preferred_element_type=F32)
        o = o + jnp.dot(a_heads[hd].astype(BF16), v, preferred_element_type=F32)
        y_ref[:, D_CONV + hd * GLA_DV:D_CONV + (hd + 1) * GLA_DV] = (
            _gla_out(o, go, gg_ref).astype(y_ref.dtype))
        kdec_t = kdec[:, kc].T
        k_blk = jnp.where(kmask, jnp.tile(kdec_t, (nseq, 1)), 0.0).astype(BF16)
        upd = jnp.dot(k_blk, v, preferred_element_type=F32)
        dec_t = dec_last[:, kc].T
        for n in range(nseq):
            dec_col = dec_t[:, n * seq_len:n * seq_len + 1]
            snew_ref[n, hd] = (s[n * GLA_DK:(n + 1) * GLA_DK, :] * dec_col
                               + upd[n * GLA_DK:(n + 1) * GLA_DK, :])


def _mixer_step(proj, glog, cw, cg, gg, state_conv, state_gla, y_buf, new_conv, new_gla, layer, *,
                row0, n_seq, seq_len):
    rows = SAMPLE_SEQS * seq_len
    base = row0 // rows
    assert row0 % rows == 0 and n_seq % SAMPLE_SEQS == 0
    kern = functools.partial(_mixer_step_kernel, nseq=SAMPLE_SEQS, seq_len=seq_len)
    return pl.pallas_call(
        kern,
        grid=(n_seq // SAMPLE_SEQS,),
        in_specs=[
            pl.BlockSpec((rows, D_PROJ), lambda i: (base + i, 0)),
            pl.BlockSpec((rows, D_QK), lambda i: (base + i, 0)),
            pl.BlockSpec((None, CONV_WIDTH, D_CONV), lambda i: (layer, 0, 0)),
            pl.BlockSpec((None, 1, D_CONV), lambda i: (layer, 0, 0)),
            pl.BlockSpec((None, 1, GLA_DV), lambda i: (layer, 0, 0)),
            pl.BlockSpec((None, SAMPLE_SEQS, SUBLANES, D_CONV), lambda i: (layer, i, 0, 0)),
            pl.BlockSpec((None, SAMPLE_SEQS, GLA_HEADS, GLA_DK, GLA_DV), lambda i: (layer, i, 0, 0, 0)),
        ] + [pl.BlockSpec(memory_space=pl.ANY)] * 3,
        out_specs=[
            pl.BlockSpec((rows, D_MODEL), lambda i: (base + i, 0)),
            pl.BlockSpec((None, SAMPLE_SEQS, SUBLANES, D_CONV), lambda i: (layer, i, 0, 0)),
            pl.BlockSpec((None, SAMPLE_SEQS, GLA_HEADS, GLA_DK, GLA_DV), lambda i: (layer, i, 0, 0, 0)),
        ],
        out_shape=[
            jax.ShapeDtypeStruct(y_buf.shape, y_buf.dtype),
            jax.ShapeDtypeStruct(state_conv.shape, F32),
            jax.ShapeDtypeStruct(state_gla.shape, F32),
        ],
        input_output_aliases={7: 0, 8: 1, 9: 2},
        compiler_params=pltpu.CompilerParams(
            dimension_semantics=("arbitrary",), vmem_limit_bytes=VMEM_LIMIT),
        name="mixer_step",
    )(proj, glog, cw, cg, gg, state_conv, state_gla, y_buf, new_conv, new_gla)


def _mixout_kernel(h_ref, y_ref, w_ref, o_ref):
    o_ref[...] = h_ref[...] + jnp.dot(y_ref[...], w_ref[...], preferred_element_type=F32)


def _mixout(h, y, w_out, layer):
    n_tok = h.shape[0]
    return pl.pallas_call(
        _mixout_kernel,
        grid=(n_tok // TM_OUT,),
        in_specs=[
            pl.BlockSpec((TM_OUT, D_MODEL), lambda i: (i, 0)),
            pl.BlockSpec((TM_OUT, D_MODEL), lambda i: (i, 0)),
            pl.BlockSpec((None, D_MODEL, D_MODEL), lambda i: (layer, 0, 0)),
        ],
        out_specs=pl.BlockSpec((TM_OUT, D_MODEL), lambda i: (i, 0)),
        out_shape=jax.ShapeDtypeStruct((n_tok, D_MODEL), F32),
        compiler_params=pltpu.CompilerParams(
            dimension_semantics=("arbitrary",), vmem_limit_bytes=VMEM_LIMIT),
        name="mixout",
    )(h, y, w_out)


def _final_norm_kernel(h_ref, g_ref, o_ref):
    h = h_ref[...]
    o_ref[...] = h * _rms_scale(h, D_MODEL) * g_ref[...]


def _final_norm(h, gain, *, row0, n_rows, tile):
    base = row0 // tile
    assert row0 % tile == 0 and n_rows % tile == 0
    return pl.pallas_call(
        _final_norm_kernel,
        grid=(n_rows // tile,),
        in_specs=[
            pl.BlockSpec((tile, D_MODEL), lambda i: (base + i, 0)),
            pl.BlockSpec((1, D_MODEL), lambda i: (0, 0)),
        ],
        out_specs=pl.BlockSpec((tile, D_MODEL), lambda i: (i, 0)),
        out_shape=jax.ShapeDtypeStruct((n_rows, D_MODEL), F32),
        compiler_params=pltpu.CompilerParams(
            dimension_semantics=("arbitrary",), vmem_limit_bytes=VMEM_LIMIT),
        name="final_norm",
    )(h, gain)


def _pad_conv_state(c):
    return jnp.pad(c, ((0, 0),) * (c.ndim - 2) + ((SUBLANES - (CONV_WIDTH - 1), 0), (0, 0)))


def kernel(x_prompt, x_sample, state_conv, state_gla, meta_tokens, norm_ffn1, w_ffn1_gu, w_ffn1_down,
           norm_mix, w_mix_in, conv_w, conv_norm, gla_fgate_w2, gla_fgate_b, gla_out_norm, w_mix_out,
           norm_ffn2, w_ffn2_gu, w_ffn2_down, norm_final):
    assert x_prompt.shape == (BATCH, SEQ, D_MODEL) and x_sample.shape == (DEC_BATCH, DEC_SEQ, D_MODEL)
    h = jnp.concatenate([
        x_prompt.reshape(N_PROMPT, D_MODEL),
        x_sample.reshape(N_SAMPLE, D_MODEL),
        jnp.tile(meta_tokens.astype(x_prompt.dtype), (BATCH, 1)),
    ], axis=0)

    wgu1, wd1, wgu2, wd2 = w_ffn1_gu, w_ffn1_down, w_ffn2_gu, w_ffn2_down
    w_in = w_mix_in.astype(BF16)
    fw2 = jnp.pad(gla_fgate_w2, ((0, 0), (0, LANES - GLA_GATE_RANK), (0, 0))).astype(BF16)
    w_out = w_mix_out.astype(BF16)
    n1, nm, n2 = (g.reshape(DEPTH, 1, D_MODEL) for g in (norm_ffn1, norm_mix, norm_ffn2))
    fb = gla_fgate_b.reshape(DEPTH, 1, D_QK)
    mix_w = (conv_w, conv_norm.reshape(DEPTH, 1, D_CONV), gla_out_norm.reshape(DEPTH, 1, GLA_DV))
    conv_s_in = _pad_conv_state(state_conv)

    zero_conv = jnp.zeros((BATCH, SUBLANES, D_CONV), F32)
    zero_gla = jnp.zeros((BATCH, GLA_HEADS, GLA_DK, GLA_DV), F32)
    y = jnp.zeros((N_TOK, D_MODEL), BF16)
    conv_s = jnp.zeros(conv_s_in.shape, F32)
    gla_s = jnp.zeros(state_gla.shape, F32)
    conv_p, gla_p = [], []
    for l in range(DEPTH):
        h = _ffn(h, n1, wgu1, wd1, l)
        proj, glog = _mixin(h, nm, w_in, fw2, fb, l)
        y, c_m, s_m = _mixer_chain(proj, glog, *mix_w, zero_conv, zero_gla, y, l, row0=ROW_META,
                                   n_seq=BATCH, seq_len=N_META, chunk=N_META, block=N_META)
        y, c_p, s_p = _mixer_chain(proj, glog, *mix_w, c_m, s_m, y, l, row0=0,
                                   n_seq=BATCH, seq_len=SEQ, chunk=GLA_CHUNK, block=MIX_BLOCK)
        y, conv_s, gla_s = _mixer_step(proj, glog, *mix_w, conv_s_in, state_gla, y, conv_s, gla_s, l,
                                       row0=ROW_SAMPLE, n_seq=DEC_BATCH, seq_len=DEC_SEQ)
        h = _mixout(h, y, w_out, l)
        h = _ffn(h, n2, wgu2, wd2, l)
        conv_p.append(c_p[:, SUBLANES - (CONV_WIDTH - 1):])
        gla_p.append(s_p)

    y_prompt = _final_norm(h, norm_final[None], row0=0, n_rows=N_PROMPT, tile=512)
    y_sample = _final_norm(h, norm_final[None], row0=ROW_SAMPLE, n_rows=N_SAMPLE, tile=512)
    return (y_prompt.reshape(BATCH, SEQ, D_MODEL), y_sample.reshape(DEC_BATCH, DEC_SEQ, D_MODEL),
            jnp.stack(gla_p), jnp.stack(conv_p), gla_s, conv_s[:, :, SUBLANES - (CONV_WIDTH - 1):])
```

```python
import functools

import jax
import jax.numpy as jnp
from jax import lax
from jax.experimental import pallas as pl
from jax.experimental.pallas import tpu as pltpu

D_MODEL = 2048
BATCH = 4
SEQ = 2048
DEPTH = 4
DEC_BATCH = 128
DEC_SEQ = 8
N_META = 16
D_CONV = 1024
CONV_WIDTH = 3
CONV_GROUPS = 8
CONV_GROUP_WIDTH = D_CONV // CONV_GROUPS
GLA_HEADS = 4
GLA_DV = 256
GLA_DK = 128
GLA_GATE_RANK = 16
GLA_GATE_TAU = 16.0
GLA_CHUNK = 64
D_FF = 5632
EPS = 1e-6
D_QK = GLA_HEADS * GLA_DK
D_V = GLA_HEADS * GLA_DV
D_PROJ = 3 * D_CONV + 2 * D_QK + 2 * D_V

OFF_CB, OFF_CC, OFF_CH = 0, D_CONV, 2 * D_CONV
OFF_Q = 3 * D_CONV
OFF_K = OFF_Q + D_QK
OFF_V = OFF_K + D_QK
OFF_GO = OFF_V + D_V

N_PROMPT = BATCH * SEQ
N_SAMPLE = DEC_BATCH * DEC_SEQ
N_METAROWS = BATCH * N_META
ROW_SAMPLE = N_PROMPT
ROW_META = N_PROMPT + N_SAMPLE
N_TOK = ROW_META + N_METAROWS

SUBLANES = 8
LANES = 128
VMEM_LIMIT = 63 * 1024 * 1024

TM = 928
TF = 512
TN_MIX = 1024
TM_OUT = 464
SUB = 16
MAX_FACTORED_DECAY = 40.0
MIX_BLOCK = 256
SAMPLE_SEQS = 8

F32 = jnp.float32
BF16 = jnp.bfloat16


def _rms_scale(x, width):
    return lax.rsqrt(jnp.sum(x * x, axis=-1, keepdims=True) * (1.0 / width) + EPS)


def _sigmoid(x):
    return 1.0 / (1.0 + jnp.exp(-x))


def _ffn_kernel(h_ref, g_ref, wg_ref, wu_ref, wd_ref, o_ref, xn_ref):
    j = pl.program_id(1)

    @pl.when(j == 0)
    def _():
        h = h_ref[...]
        xn_ref[...] = (h * _rms_scale(h, D_MODEL) * g_ref[...]).astype(BF16)
        o_ref[...] = h

    xn = xn_ref[...]
    a = jnp.dot(xn, wg_ref[...].astype(BF16), preferred_element_type=F32)
    b = jnp.dot(xn, wu_ref[...].astype(BF16), preferred_element_type=F32)
    act = (a * _sigmoid(a) * (b * 0.5)).astype(BF16)
    o_ref[...] += jnp.dot(act, wd_ref[...].astype(BF16), preferred_element_type=F32)


def _ffn(h, gain, w_gu, w_down, layer):
    n_tok = h.shape[0]
    nf = D_FF // TF
    return pl.pallas_call(
        _ffn_kernel,
        grid=(n_tok // TM, nf),
        in_specs=[
            pl.BlockSpec((TM, D_MODEL), lambda i, j: (i, 0)),
            pl.BlockSpec((None, 1, D_MODEL), lambda i, j: (layer, 0, 0)),
            pl.BlockSpec((None, D_MODEL, TF), lambda i, j: (layer, 0, j)),
            pl.BlockSpec((None, D_MODEL, TF), lambda i, j: (layer, 0, j + nf)),
            pl.BlockSpec((None, TF, D_MODEL), lambda i, j: (layer, j, 0)),
        ],
        out_specs=pl.BlockSpec((TM, D_MODEL), lambda i, j: (i, 0)),
        out_shape=jax.ShapeDtypeStruct((n_tok, D_MODEL), F32),
        scratch_shapes=[pltpu.VMEM((TM, D_MODEL), BF16)],
        compiler_params=pltpu.CompilerParams(
            dimension_semantics=("arbitrary", "arbitrary"), vmem_limit_bytes=VMEM_LIMIT),
        name="ffn",
    )(h, gain, w_gu, w_gu, w_down)


def _mixin_kernel(h_ref, g_ref, w_ref, wfl_ref, fw2_ref, fb_ref, proj_ref, glog_ref, xn_ref):
    j = pl.program_id(1)

    @pl.when(j == 0)
    def _():
        h = h_ref[...]
        xn = (h * _rms_scale(h, D_MODEL) * g_ref[...]).astype(BF16)
        xn_ref[...] = xn
        fl = jnp.dot(xn, wfl_ref[...].astype(BF16), preferred_element_type=F32)
        fl = jnp.where(lax.broadcasted_iota(jnp.int32, fl.shape, 1) < GLA_GATE_RANK, fl, 0.0)
        z =jnp.dot(fl.astype(BF16), fw2_ref[...], preferred_element_type=F32) + fb_ref[...]
        glog_ref[...] = (jnp.minimum(z, 0.0) - jnp.log1p(jnp.exp(-jnp.abs(z)))) * (1.0 / GLA_GATE_TAU)

    proj_ref[...] = jnp.dot(xn_ref[...], w_ref[...].astype(BF16), preferred_element_type=F32)


def _mixin(h, gain, w_in, fw2, fb, layer):
    n_tok = h.shape[0]
    return pl.pallas_call(
        _mixin_kernel,
        grid=(n_tok // TM, D_PROJ // TN_MIX),
        in_specs=[
            pl.BlockSpec((TM, D_MODEL), lambda i, j: (i, 0)),
            pl.BlockSpec((None, 1, D_MODEL), lambda i, j: (layer, 0, 0)),
            pl.BlockSpec((None, D_MODEL, TN_MIX), lambda i, j: (layer, 0, j)),
            pl.BlockSpec((None, D_MODEL, LANES), lambda i, j: (layer, 0, D_PROJ // LANES)),
            pl.BlockSpec((None, LANES, D_QK), lambda i, j: (layer, 0, 0)),
            pl.BlockSpec((None, 1, D_QK), lambda i, j: (layer, 0, 0)),
        ],
        out_specs=[
            pl.BlockSpec((TM, TN_MIX), lambda i, j: (i, j)),
            pl.BlockSpec((TM, D_QK), lambda i, j: (i, 0)),
        ],
        out_shape=[
            jax.ShapeDtypeStruct((n_tok, D_PROJ), F32),
            jax.ShapeDtypeStruct((n_tok, D_QK), F32),
        ],
        scratch_shapes=[pltpu.VMEM((TM, D_MODEL), BF16)],
        compiler_params=pltpu.CompilerParams(
            dimension_semantics=("arbitrary", "arbitrary"), vmem_limit_bytes=VMEM_LIMIT),
        name="mixin",
    )(h, gain, w_in, w_in, fw2, fb)


def _block_rows(x, block, j):
    rows, width = x.shape
    if rows == block:
        return jnp.broadcast_to(x[j:j + 1, :], (rows, width))
    x3 = x.reshape(rows // block, block, width)
    return jnp.broadcast_to(x3[:, j:j + 1, :], (rows // block, block, width)).reshape(rows, width)


def _conv_branch(cb, u, u1, u2, cw_ref, cg_ref, y_ref, rows):
    conv = cw_ref[0:1, :] * u2 + cw_ref[1:2, :] * u1 + cw_ref[2:3, :] * u
    z = cb * conv
    for grp in range(CONV_GROUPS):
        cols = slice(grp * CONV_GROUP_WIDTH, (grp + 1) * CONV_GROUP_WIDTH)
        zg = z[:, cols]
        y_ref[rows, cols] = (zg * _rms_scale(zg, CONV_GROUP_WIDTH) * cg_ref[:, cols]).astype(y_ref.dtype)


def _gla_rows(q, k, g, seg, sub, single_ref):
    rows = q.shape[0]
    ri = lax.broadcasted_iota(jnp.int32, (rows, rows), 0)
    ci = lax.broadcasted_iota(jnp.int32, (rows, rows), 1)
    causal = ci <= ri
    if rows != seg:
        causal = causal & ((ri // seg) == (ci // seg))
    mcum = jnp.where(causal, 1.0, 0.0).astype(F32)
    b = jnp.dot(mcum, g, preferred_element_type=F32, precision=lax.Precision.HIGHEST)
    qs = q * (GLA_DK ** -0.5)
    b_last = _block_rows(b, seg, seg - 1)
    qb = qs * jnp.exp(b)
    dec_last = jnp.exp(b_last)

    if single_ref:
        kinv = k * jnp.exp(-b)
        kdec = kinv * dec_last
        qb16, kinv16 = qb.astype(BF16), kinv.astype(BF16)
        a_heads = []
        for hd in range(GLA_HEADS):
            cols = slice(hd * GLA_DK, (hd + 1) * GLA_DK)
            a_h = lax.dot_general(qb16[:, cols], kinv16[:, cols], (((1,), (1,)), ((), ())),
                                  preferred_element_type=F32)
            a_heads.append(jnp.where(causal, a_h, 0.0))
        return qb, kdec, dec_last, a_heads

    blk0 = (ri // sub) * sub
    dcol = jnp.where(causal & (ci >= blk0), ci - blk0, -1)
    a_heads = [jnp.zeros((rows, rows), F32) for _ in range(GLA_HEADS)]
    for j in range(sub):
        kb = _block_rows(k, sub, j)
        bb = _block_rows(b, sub, j)
        p = qs * kb * jnp.exp(jnp.minimum(b - bb, 0.0))
        hit = dcol == j
        for hd in range(GLA_HEADS):
            r = jnp.sum(p[:, hd * GLA_DK:(hd + 1) * GLA_DK], axis=-1, keepdims=True)
            a_heads[hd] = jnp.where(hit, r, a_heads[hd])

    nsub = seg // sub
    if nsub > 1:
        assert rows == seg
        off = [[jnp.zeros((sub, rows), F32)] for _ in range(GLA_HEADS)]
        ci_s = lax.broadcasted_iota(jnp.int32, (sub, rows), 1)
        for i in range(1, nsub):
            lo = i * sub
            bs = b[lo - 1:lo, :]
            qd = (qs[lo:lo + sub, :] * jnp.exp(b[lo:lo + sub, :] - bs)).astype(BF16)
            kd = (k * jnp.exp(jnp.minimum(bs - b, 0.0))).astype(BF16)
            for hd in range(GLA_HEADS):
                cols = slice(hd * GLA_DK, (hd + 1) * GLA_DK)
                a_i = lax.dot_general(qd[:, cols], kd[:, cols], (((1,), (1,)), ((), ())),
                                      preferred_element_type=F32)
                off[hd].append(jnp.where(ci_s < lo, a_i, 0.0))
        a_heads = [a_heads[hd] + jnp.concatenate(off[hd], axis=0) for hd in range(GLA_HEADS)]

    kdec = k * jnp.exp(b_last - b)
    return qb, kdec, dec_last, a_heads


def _gla_out(o, go, gg_ref):
    return o * _rms_scale(o, GLA_DV) * gg_ref[...] * (go * _sigmoid(go))


def _mixer_chain_kernel(proj_ref, glog_ref, cw_ref, cg_ref, gg_ref, cprev_ref, s0_ref, y_in_ref,
                        y_ref, cnew_ref, snew_ref, s_scr, tail_scr, *, chunk, nchunks):
    del y_in_ref
    blk = pl.program_id(1)
    tb = chunk * nchunks
    sub = min(SUB, chunk)

    @pl.when(blk == 0)
    def _():
        s_scr[...] = s0_ref[0]
        tail_scr[...] = cprev_ref[0]

    u = proj_ref[:, OFF_CC:OFF_CC + D_CONV] * proj_ref[:, OFF_CH:OFF_CH + D_CONV]
    prev = tail_scr[...]
    t8 = lax.broadcasted_iota(jnp.int32, (SUBLANES, D_CONV), 0)
    r1 = pltpu.roll(u, 1, 0)
    r2 = pltpu.roll(u, 2, 0)
    f1 = jnp.where(t8 < 1, pltpu.roll(prev, 1, 0), r1[0:SUBLANES])
    f2 = jnp.where(t8 < 2, pltpu.roll(prev, 2, 0), r2[0:SUBLANES])
    if tb > SUBLANES:
        u1 = jnp.concatenate([f1, r1[SUBLANES:]], axis=0)
        u2 = jnp.concatenate([f2, r2[SUBLANES:]], axis=0)
    else:
        u1, u2 = f1, f2
    tail_scr[...] = u[tb - SUBLANES:tb]
    _conv_branch(proj_ref[:, OFF_CB:OFF_CB + D_CONV], u, u1, u2, cw_ref, cg_ref, y_ref, slice(None))

    def chunk_step(single_ref, c):
        rows = slice(c * chunk, (c + 1) * chunk)
        q = proj_ref[rows, OFF_Q:OFF_Q + D_QK]
        k = proj_ref[rows, OFF_K:OFF_K + D_QK]
        g = glog_ref[rows, :]
        qb, kdec, dec_last, a_heads = _gla_rows(q, k, g, chunk, sub, single_ref)
        for hd in range(GLA_HEADS):
            kc = slice(hd * GLA_DK, (hd + 1) * GLA_DK)
            v = proj_ref[rows, OFF_V + hd * GLA_DV:OFF_V + (hd + 1) * GLA_DV].astype(BF16)
            go = proj_ref[rows, OFF_GO + hd * GLA_DV:OFF_GO + (hd + 1) * GLA_DV]
            s = s_scr[hd]
            o = jnp.dot(qb[:, kc].astype(BF16), s.astype(BF16), preferred_element_type=F32)
            o = o + jnp.dot(a_heads[hd].astype(BF16), v, preferred_element_type=F32)
            y_ref[rows, D_CONV + hd * GLA_DV:D_CONV + (hd + 1) * GLA_DV] = (
                _gla_out(o, go, gg_ref).astype(y_ref.dtype))
            kdec_t = kdec[:, kc].T.astype(BF16)
            dec_col = jnp.broadcast_to(dec_last[0:1, kc], (SUBLANES, GLA_DK)).T[:, 0:1]
            s_scr[hd] = s * dec_col + jnp.dot(kdec_t, v, preferred_element_type=F32)

    def run_chunks(single_ref):
        for c in range(nchunks):
            chunk_step(single_ref, c)

    if sub == chunk:
        run_chunks(False)
    else:
        decay_bound = -chunk * jnp.min(glog_ref[...])
        lax.cond(decay_bound <= MAX_FACTORED_DECAY,
                 functools.partial(run_chunks, True), functools.partial(run_chunks, False))

    @pl.when(blk == pl.num_programs(1) - 1)
    def _():
        snew_ref[0] = s_scr[...]
        cnew_ref[0] = tail_scr[...]


def _mixer_chain(proj, glog, cw, cg, gg, cprev, s0, y_buf, layer, *, row0, n_seq, seq_len, chunk, block):
    nblk = seq_len // block
    base = row0 // block
    assert row0 % block == 0 and seq_len % block == 0 and block % chunk == 0
    kern = functools.partial(_mixer_chain_kernel, chunk=chunk, nchunks=block // chunk)
    return pl.pallas_call(
        kern,
        grid=(n_seq, nblk),
        in_specs=[
            pl.BlockSpec((block, D_PROJ), lambda s, i: (base + s * nblk + i, 0)),
            pl.BlockSpec((block, D_QK), lambda s, i: (base + s * nblk + i, 0)),
            pl.BlockSpec((None, CONV_WIDTH, D_CONV), lambda s, i: (layer, 0, 0)),
            pl.BlockSpec((None, 1, D_CONV), lambda s, i: (layer, 0, 0)),
            pl.BlockSpec((None, 1, GLA_DV), lambda s, i: (layer, 0, 0)),
            pl.BlockSpec((1, SUBLANES, D_CONV), lambda s, i: (s, 0, 0)),
            pl.BlockSpec((1, GLA_HEADS, GLA_DK, GLA_DV), lambda s, i: (s, 0, 0, 0)),
            pl.BlockSpec(memory_space=pl.ANY),
        ],
        out_specs=[
            pl.BlockSpec((block, D_MODEL), lambda s, i: (base + s * nblk + i, 0)),
            pl.BlockSpec((1, SUBLANES, D_CONV), lambda s, i: (s, 0, 0)),
            pl.BlockSpec((1, GLA_HEADS, GLA_DK, GLA_DV), lambda s, i: (s, 0, 0, 0)),
        ],
        out_shape=[
            jax.ShapeDtypeStruct(y_buf.shape, y_buf.dtype),
            jax.ShapeDtypeStruct((n_seq, SUBLANES, D_CONV), F32),
            jax.ShapeDtypeStruct((n_seq, GLA_HEADS, GLA_DK, GLA_DV), F32),
        ],
        scratch_shapes=[
            pltpu.VMEM((GLA_HEADS, GLA_DK, GLA_DV), F32),
            pltpu.VMEM((SUBLANES, D_CONV), F32),
        ],
        input_output_aliases={7: 0},
        compiler_params=pltpu.CompilerParams(
            dimension_semantics=("arbitrary", "arbitrary"), vmem_limit_bytes=VMEM_LIMIT),
        name=f"mixer_chain{chunk}",
    )(proj, glog, cw, cg, gg, cprev, s0, y_buf)


def _mixer_step_kernel(proj_ref, glog_ref, cw_ref, cg_ref, gg_ref, cprev_ref, s0_ref,
                       y_in_ref, cnew_in_ref, snew_in_ref, y_ref, cnew_ref, snew_ref, *, nseq, seq_len):
    del y_in_ref, cnew_in_ref, snew_in_ref
    rows = nseq * seq_len
    assert seq_len == SUBLANES

    u = proj_ref[:, OFF_CC:OFF_CC + D_CONV] * proj_ref[:, OFF_CH:OFF_CH + D_CONV]
    u3 = u.reshape(nseq, seq_len, D_CONV)
    prev = cprev_ref[...]
    t8 = lax.broadcasted_iota(jnp.int32, (nseq, seq_len, D_CONV), 1)
    u1 = jnp.where(t8 < 1, pltpu.roll(prev, 1, 1), pltpu.roll(u3, 1, 1)).reshape(rows, D_CONV)
    u2 = jnp.where(t8 < 2, pltpu.roll(prev, 2, 1), pltpu.roll(u3, 2, 1)).reshape(rows, D_CONV)
    cnew_ref[...] = u3
    _conv_branch(proj_ref[:, OFF_CB:OFF_CB + D_CONV], u, u1, u2, cw_ref, cg_ref, y_ref, slice(None))

    q = proj_ref[:, OFF_Q:OFF_Q + D_QK]
    k = proj_ref[:, OFF_K:OFF_K + D_QK]
    g = glog_ref[...]
    qb, kdec, dec_last, a_heads = _gla_rows(q, k, g, seq_len, seq_len, False)

    qmask = (lax.broadcasted_iota(jnp.int32, (rows, nseq * GLA_DK), 0) // seq_len
             == lax.broadcasted_iota(jnp.int32, (rows, nseq * GLA_DK), 1) // GLA_DK)
    kmask = (lax.broadcasted_iota(jnp.int32, (nseq * GLA_DK, rows), 0) // GLA_DK
             == lax.broadcasted_iota(jnp.int32, (nseq * GLA_DK, rows), 1) // seq_len)
    for hd in range(GLA_HEADS):
        kc = slice(hd * GLA_DK, (hd + 1) * GLA_DK)
        v = proj_ref[:, OFF_V + hd * GLA_DV:OFF_V + (hd + 1) * GLA_DV].astype(BF16)
        go = proj_ref[:, OFF_GO + hd * GLA_DV:OFF_GO + (hd + 1) * GLA_DV]
        s = s0_ref[:, hd].reshape(nseq * GLA_DK, GLA_DV)
        q_blk = jnp.where(qmask, jnp.tile(qb[:, kc], (1, nseq)), 0.0).astype(BF16)
        o = jnp.dot(q_blk, s.astype(BF16), preferred_element_type=F32)
        o = o + jnp.dot(a_heads[hd].astype(BF16), v, preferred_element_type=F32)
        y_ref[:, D_CONV + hd * GLA_DV:D_CONV + (hd + 1) * GLA_DV] = (
            _gla_out(o, go, gg_ref).astype(y_ref.dtype))
        kdec_t = kdec[:, kc].T
        k_blk = jnp.where(kmask, jnp.tile(kdec_t, (nseq, 1)), 0.0).astype(BF16)
        upd = jnp.dot(k_blk, v, preferred_element_type=F32)
        dec_t = dec_last[:, kc].T
        for n in range(nseq):
            dec_col = dec_t[:, n * seq_len:n * seq_len + 1]
            snew_ref[n, hd] = (s[n * GLA_DK:(n + 1) * GLA_DK, :] * dec_col
                               + upd[n * GLA_DK:(n + 1) * GLA_DK, :])


def _mixer_step(proj, glog, cw, cg, gg, state_conv, state_gla, y_buf, new_conv, new_gla, layer, *,
                row0, n_seq, seq_len):
    rows = SAMPLE_SEQS * seq_len
    base = row0 // rows
    assert row0 % rows == 0 and n_seq % SAMPLE_SEQS == 0
    kern = functools.partial(_mixer_step_kernel, nseq=SAMPLE_SEQS, seq_len=seq_len)
    return pl.pallas_call(
        kern,
        grid=(n_seq // SAMPLE_SEQS,),
        in_specs=[
            pl.BlockSpec((rows, D_PROJ), lambda i: (base + i, 0)),
            pl.BlockSpec((rows, D_QK), lambda i: (base + i, 0)),
            pl.BlockSpec((None, CONV_WIDTH, D_CONV), lambda i: (layer, 0, 0)),
            pl.BlockSpec((None, 1, D_CONV), lambda i: (layer, 0, 0)),
            pl.BlockSpec((None, 1, GLA_DV), lambda i: (layer, 0, 0)),
            pl.BlockSpec((None, SAMPLE_SEQS, SUBLANES, D_CONV), lambda i: (layer, i, 0, 0)),
            pl.BlockSpec((None, SAMPLE_SEQS, GLA_HEADS, GLA_DK, GLA_DV), lambda i: (layer, i, 0, 0, 0)),
        ] + [pl.BlockSpec(memory_space=pl.ANY)] * 3,
        out_specs=[
            pl.BlockSpec((rows, D_MODEL), lambda i: (base + i, 0)),
            pl.BlockSpec((None, SAMPLE_SEQS, SUBLANES, D_CONV), lambda i: (layer, i, 0, 0)),
            pl.BlockSpec((None, SAMPLE_SEQS, GLA_HEADS, GLA_DK, GLA_DV), lambda i: (layer, i, 0, 0, 0)),
        ],
        out_shape=[
            jax.ShapeDtypeStruct(y_buf.shape, y_buf.dtype),
            jax.ShapeDtypeStruct(state_conv.shape, F32),
            jax.ShapeDtypeStruct(state_gla.shape, F32),
        ],
        input_output_aliases={7: 0, 8: 1, 9: 2},
        compiler_params=pltpu.CompilerParams(
            dimension_semantics=("arbitrary",), vmem_limit_bytes=VMEM_LIMIT),
        name="mixer_step",
    )(proj, glog, cw, cg, gg, state_conv, state_gla, y_buf, new_conv, new_gla)


def _mixout_kernel(h_ref, y_ref, w_ref, o_ref):
    o_ref[...] = h_ref[...] + jnp.dot(y_ref[...], w_ref[...].astype(BF16), preferred_element_type=F32)


def _mixout(h, y, w_out, layer):
    n_tok = h.shape[0]
    return pl.pallas_call(
        _mixout_kernel,
        grid=(n_tok // TM_OUT,),
        in_specs=[
            pl.BlockSpec((TM_OUT, D_MODEL), lambda i: (i, 0)),
            pl.BlockSpec((TM_OUT, D_MODEL), lambda i: (i, 0)),
            pl.BlockSpec((None, D_MODEL, D_MODEL), lambda i: (layer, 0, 0), pipeline_mode=pl.Buffered(1)),
        ],
        out_specs=pl.BlockSpec((TM_OUT, D_MODEL), lambda i: (i, 0)),
        out_shape=jax.ShapeDtypeStruct((n_tok, D_MODEL), F32),
        compiler_params=pltpu.CompilerParams(
            dimension_semantics=("arbitrary",), vmem_limit_bytes=VMEM_LIMIT),
        name="mixout",
    )(h, y, w_out)


def _final_norm_kernel(h_ref, g_ref, o_ref):
    h = h_ref[...]
    o_ref[...] = h * _rms_scale(h, D_MODEL) * g_ref[...]


def _final_norm(h, gain, *, row0, n_rows, tile):
    base = row0 // tile
    assert row0 % tile == 0 and n_rows % tile == 0
    return pl.pallas_call(
        _final_norm_kernel,
        grid=(n_rows // tile,),
        in_specs=[
            pl.BlockSpec((tile, D_MODEL), lambda i: (base + i, 0)),
            pl.BlockSpec((1, D_MODEL), lambda i: (0, 0)),
        ],
        out_specs=pl.BlockSpec((tile, D_MODEL), lambda i: (i, 0)),
        out_shape=jax.ShapeDtypeStruct((n_rows, D_MODEL), F32),
        compiler_params=pltpu.CompilerParams(
            dimension_semantics=("arbitrary",), vmem_limit_bytes=VMEM_LIMIT),
        name="final_norm",
    )(h, gain)


def _pad_conv_state(c):
    return jnp.pad(c, ((0, 0),) * (c.ndim - 2) + ((SUBLANES - (CONV_WIDTH - 1), 0), (0, 0)))


def kernel(x_prompt, x_sample, state_conv, state_gla, meta_tokens, norm_ffn1, w_ffn1_gu, w_ffn1_down,
           norm_mix, w_mix_in, conv_w, conv_norm, gla_fgate_w2, gla_fgate_b, gla_out_norm, w_mix_out,
           norm_ffn2, w_ffn2_gu, w_ffn2_down, norm_final):
    assert x_prompt.shape == (BATCH, SEQ, D_MODEL) and x_sample.shape == (DEC_BATCH, DEC_SEQ, D_MODEL)
    h = jnp.concatenate([
        x_prompt.reshape(N_PROMPT, D_MODEL),
        x_sample.reshape(N_SAMPLE, D_MODEL),
        jnp.tile(meta_tokens.astype(x_prompt.dtype), (BATCH, 1)),
    ], axis=0)

    wgu1, wd1, wgu2, wd2, w_in, w_out = w_ffn1_gu, w_ffn1_down, w_ffn2_gu, w_ffn2_down, w_mix_in, w_mix_out
    fw2 = jnp.pad(gla_fgate_w2, ((0, 0), (0, LANES - GLA_GATE_RANK), (0, 0))).astype(BF16)
    n1, nm, n2 = (g.reshape(DEPTH, 1, D_MODEL) for g in (norm_ffn1, norm_mix, norm_ffn2))
    fb = gla_fgate_b.reshape(DEPTH, 1, D_QK)
    mix_w = (conv_w, conv_norm.reshape(DEPTH, 1, D_CONV), gla_out_norm.reshape(DEPTH, 1, GLA_DV))
    conv_s_in = _pad_conv_state(state_conv)

    zero_conv = jnp.zeros((BATCH, SUBLANES, D_CONV), F32)
    zero_gla = jnp.zeros((BATCH, GLA_HEADS, GLA_DK, GLA_DV), F32)
    y = jnp.zeros((N_TOK, D_MODEL), BF16)
    conv_s = jnp.zeros(conv_s_in.shape, F32)
    gla_s = jnp.zeros(state_gla.shape, F32)
    conv_p, gla_p = [], []
    for l in range(DEPTH):
        h = _ffn(h, n1, wgu1, wd1, l)
        proj, glog = _mixin(h, nm, w_in, fw2, fb, l)
        y, c_m, s_m = _mixer_chain(proj, glog, *mix_w, zero_conv, zero_gla, y, l, row0=ROW_META,
                                   n_seq=BATCH, seq_len=N_META, chunk=N_META, block=N_META)
        y, c_p, s_p = _mixer_chain(proj, glog, *mix_w, c_m, s_m, y, l, row0=0,
                                   n_seq=BATCH, seq_len=SEQ, chunk=GLA_CHUNK, block=MIX_BLOCK)
        y, conv_s, gla_s = _mixer_step(proj, glog, *mix_w, conv_s_in, state_gla, y, conv_s, gla_s, l,
                                       row0=ROW_SAMPLE, n_seq=DEC_BATCH, seq_len=DEC_SEQ)
        h = _mixout(h, y, w_out, l)
        h = _ffn(h, n2, wgu2, wd2, l)
        conv_p.append(c_p[:, SUBLANES - (CONV_WIDTH - 1):])
        gla_p.append(s_p)

    y_prompt = _final_norm(h, norm_final[None], row0=0, n_rows=N_PROMPT, tile=512)
    y_sample = _final_norm(h, norm_final[None], row0=ROW_SAMPLE, n_rows=N_SAMPLE, tile=512)
    return (y_prompt.reshape(BATCH, SEQ, D_MODEL), y_sample.reshape(DEC_BATCH, DEC_SEQ, D_MODEL),
            jnp.stack(gla_p), jnp.stack(conv_p), gla_s, conv_s[:, :, SUBLANES - (CONV_WIDTH - 1):])
```

```python
import functools

import jax
import jax.numpy as jnp
from jax import lax
from jax.experimental import pallas as pl
from jax.experimental.pallas import tpu as pltpu

D_MODEL = 2048
BATCH = 4
SEQ = 2048
DEPTH = 4
DEC_BATCH = 128
DEC_SEQ = 8
N_META = 16
D_CONV = 1024
CONV_WIDTH = 3
CONV_GROUPS = 8
CONV_GROUP_WIDTH = D_CONV // CONV_GROUPS
GLA_HEADS = 4
GLA_DV = 256
GLA_DK = 128
GLA_GATE_RANK = 16
GLA_GATE_TAU = 16.0
GLA_CHUNK = 64
D_FF = 5632
EPS = 1e-6
D_QK = GLA_HEADS * GLA_DK
D_V = GLA_HEADS * GLA_DV
D_PROJ = 3 * D_CONV + 2 * D_QK + 2 * D_V

OFF_CB, OFF_CC, OFF_CH = 0, D_CONV, 2 * D_CONV
OFF_Q = 3 * D_CONV
OFF_K = OFF_Q + D_QK
OFF_V = OFF_K + D_QK
OFF_GO = OFF_V + D_V

N_PROMPT = BATCH * SEQ
N_SAMPLE = DEC_BATCH * DEC_SEQ
N_METAROWS = BATCH * N_META
ROW_SAMPLE = N_PROMPT
ROW_META = N_PROMPT + N_SAMPLE
N_TOK = ROW_META + N_METAROWS

SUBLANES = 8
LANES = 128
VMEM_LIMIT = 63 * 1024 * 1024

TM = 928
TF = 512
TN_MIX = 1024
TM_OUT = 464
SUB = 16
MAX_FACTORED_DECAY = 40.0
MIX_BLOCK = 256
SAMPLE_SEQS = 8

F32 = jnp.float32
BF16 = jnp.bfloat16


def _rms_scale(x, width):
    return lax.rsqrt(jnp.sum(x * x, axis=-1, keepdims=True) * (1.0 / width) + EPS)


def _sigmoid(x):
    return 1.0 / (1.0 + jnp.exp(-x))


def _ffn_kernel(h_ref, g_ref, wg_ref, wu_ref, wd_ref, o_ref, xn_ref):
    j = pl.program_id(1)

    @pl.when(j == 0)
    def _():
        h = h_ref[...]
        xn_ref[...] = (h * _rms_scale(h, D_MODEL) * g_ref[...]).astype(BF16)

    xn = xn_ref[...]
    a = jnp.dot(xn, wg_ref[...].astype(BF16), preferred_element_type=F32)
    b = jnp.dot(xn, wu_ref[...].astype(BF16), preferred_element_type=F32)
    act = (a * _sigmoid(a) * (b * 0.5)).astype(BF16)
    acc = jnp.where(j == 0, h_ref[...], o_ref[...])
    o_ref[...] = acc + jnp.dot(act, wd_ref[...].astype(BF16), preferred_element_type=F32)


def _ffn(h, gain, w_gu, w_down, layer):
    n_tok = h.shape[0]
    nf = D_FF // TF
    return pl.pallas_call(
        _ffn_kernel,
        grid=(n_tok // TM, nf),
        in_specs=[
            pl.BlockSpec((TM, D_MODEL), lambda i, j: (i, 0)),
            pl.BlockSpec((None, 1, D_MODEL), lambda i, j: (layer, 0, 0)),
            pl.BlockSpec((None, D_MODEL, TF), lambda i, j: (layer, 0, j)),
            pl.BlockSpec((None, D_MODEL, TF), lambda i, j: (layer, 0, j + nf)),
            pl.BlockSpec((None, TF, D_MODEL), lambda i, j: (layer, j, 0)),
        ],
        out_specs=pl.BlockSpec((TM, D_MODEL), lambda i, j: (i, 0)),
        out_shape=jax.ShapeDtypeStruct((n_tok, D_MODEL), F32),
        scratch_shapes=[pltpu.VMEM((TM, D_MODEL), BF16)],
        compiler_params=pltpu.CompilerParams(
            dimension_semantics=("arbitrary", "arbitrary"), vmem_limit_bytes=VMEM_LIMIT),
        name="ffn",
    )(h, gain, w_gu, w_gu, w_down)


_NT = (((1,), (1,)), ((), ()))


def _mixin_kernel(h_ref, g_ref, wt_ref, wflt_ref, fw2_ref, fb_ref, proj_ref, glog_ref, xn_ref):
    j = pl.program_id(1)
    half = pl.program_id(2)

    @pl.when(j == 0)
    def _():
        h = h_ref[...]
        xn = (h * _rms_scale(h, D_MODEL) * g_ref[...]).astype(BF16)
        xn_ref[half] = xn
        fl = lax.dot_general(xn, wflt_ref[...].astype(BF16), _NT, preferred_element_type=F32)
        fl = jnp.where(lax.broadcasted_iota(jnp.int32, fl.shape, 1) < GLA_GATE_RANK, fl, 0.0)
        z = jnp.dot(fl.astype(BF16), fw2_ref[...], preferred_element_type=F32) + fb_ref[...]
        glog_ref[...] = (jnp.minimum(z, 0.0) - jnp.log1p(jnp.exp(-jnp.abs(z)))) * (1.0 / GLA_GATE_TAU)

    proj_ref[...] = lax.dot_general(xn_ref[half], wt_ref[...].astype(BF16), _NT, preferred_element_type=F32)


def _mixin(h, gain, w_in_t, fw2, fb, layer):
    n_tok = h.shape[0]
    assert n_tok % (2 * TM) == 0

    def tile(p, j, half):
        return jnp.where(j == 0, 2 * p + half, 2 * p + 1)

    return pl.pallas_call(
        _mixin_kernel,
        grid=(n_tok // (2 * TM), D_PROJ // TN_MIX, 2),
        in_specs=[
            pl.BlockSpec((TM, D_MODEL), lambda p, j, half: (tile(p, j, half), 0)),
            pl.BlockSpec((None, 1, D_MODEL), lambda p, j, half: (layer, 0, 0)),
            pl.BlockSpec((None, TN_MIX, D_MODEL), lambda p, j, half: (layer, j, 0)),
            pl.BlockSpec((None, LANES, D_MODEL), lambda p, j, half: (layer, D_PROJ // LANES, 0)),
            pl.BlockSpec((None, LANES, D_QK), lambda p, j, half: (layer, 0, 0)),
            pl.BlockSpec((None, 1, D_QK), lambda p, j, half: (layer, 0, 0)),
        ],
        out_specs=[
            pl.BlockSpec((TM, TN_MIX), lambda p, j, half: (2 * p + half, j)),
            pl.BlockSpec((TM, D_QK), lambda p, j, half: (tile(p, j, half), 0)),
        ],
        out_shape=[
            jax.ShapeDtypeStruct((n_tok, D_PROJ), F32),
            jax.ShapeDtypeStruct((n_tok, D_QK), F32),
        ],
        scratch_shapes=[pltpu.VMEM((2, TM, D_MODEL), BF16)],
        compiler_params=pltpu.CompilerParams(
            dimension_semantics=("arbitrary", "arbitrary", "arbitrary"), vmem_limit_bytes=VMEM_LIMIT),
        name="mixin",
    )(h, gain, w_in_t, w_in_t, fw2, fb)


def _block_rows(x, block, j):
    rows, width = x.shape
    if rows == block:
        return jnp.broadcast_to(x[j:j + 1, :], (rows, width))
    x3 = x.reshape(rows // block, block, width)
    return jnp.broadcast_to(x3[:, j:j + 1, :], (rows // block, block, width)).reshape(rows, width)


def _conv_branch(cb, u, u1, u2, cw_ref, cg_ref, y_ref, rows):
    conv = cw_ref[0:1, :] * u2 + cw_ref[1:2, :] * u1 + cw_ref[2:3, :] * u
    z = cb * conv
    for grp in range(CONV_GROUPS):
        cols = slice(grp * CONV_GROUP_WIDTH, (grp + 1) * CONV_GROUP_WIDTH)
        zg = z[:, cols]
        y_ref[rows, cols] = (zg * _rms_scale(zg, CONV_GROUP_WIDTH) * cg_ref[:, cols]).astype(y_ref.dtype)


def _gla_rows(q, k, g, seg, sub, single_ref):
    rows = q.shape[0]
    ri = lax.broadcasted_iota(jnp.int32, (rows, rows), 0)
    ci = lax.broadcasted_iota(jnp.int32, (rows, rows), 1)
    causal = ci <= ri
    if rows != seg:
        causal = causal & ((ri // seg) == (ci // seg))
    mcum = jnp.where(causal, 1.0, 0.0).astype(F32)
    b = jnp.dot(mcum, g, preferred_element_type=F32, precision=lax.Precision.HIGHEST)
    qs = q * (GLA_DK ** -0.5)
    b_last = _block_rows(b, seg, seg - 1)
    qb = qs * jnp.exp(b)
    dec_last = jnp.exp(b_last)

    if single_ref:
        kinv = k * jnp.exp(-b)
        kdec = kinv * dec_last
        qb16, kinv16 = qb.astype(BF16), kinv.astype(BF16)
        a_heads = []
        for hd in range(GLA_HEADS):
            cols = slice(hd * GLA_DK, (hd + 1) * GLA_DK)
            a_h = lax.dot_general(qb16[:, cols], kinv16[:, cols], (((1,), (1,)), ((), ())),
                                  preferred_element_type=F32)
            a_heads.append(jnp.where(causal, a_h, 0.0))
        return qb, kdec, dec_last, a_heads

    blk0 = (ri // sub) * sub
    dcol = jnp.where(causal & (ci >= blk0), ci - blk0, -1)
    a_heads = [jnp.zeros((rows, rows), F32) for _ in range(GLA_HEADS)]
    for j in range(sub):
        kb = _block_rows(k, sub, j)
        bb = _block_rows(b, sub, j)
        p = qs * kb * jnp.exp(jnp.minimum(b - bb, 0.0))
        hit = dcol == j
        for hd in range(GLA_HEADS):
            r = jnp.sum(p[:, hd * GLA_DK:(hd + 1) * GLA_DK], axis=-1, keepdims=True)
            a_heads[hd] = jnp.where(hit, r, a_heads[hd])

    nsub = seg // sub
    if nsub > 1:
        assert rows == seg
        off = [[jnp.zeros((sub, rows), F32)] for _ in range(GLA_HEADS)]
        ci_s = lax.broadcasted_iota(jnp.int32, (sub, rows), 1)
        for i in range(1, nsub):
            lo = i * sub
            bs = b[lo - 1:lo, :]
            qd = (qs[lo:lo + sub, :] * jnp.exp(b[lo:lo + sub, :] - bs)).astype(BF16)
            kd = (k * jnp.exp(jnp.minimum(bs - b, 0.0))).astype(BF16)
            for hd in range(GLA_HEADS):
                cols = slice(hd * GLA_DK, (hd + 1) * GLA_DK)
                a_i = lax.dot_general(qd[:, cols], kd[:, cols], (((1,), (1,)), ((), ())),
                                      preferred_element_type=F32)
                off[hd].append(jnp.where(ci_s < lo, a_i, 0.0))
        a_heads = [a_heads[hd] + jnp.concatenate(off[hd], axis=0) for hd in range(GLA_HEADS)]

    kdec = k * jnp.exp(b_last - b)
    return qb, kdec, dec_last, a_heads


def _gla_out(o, go, gg_ref):
    return o * _rms_scale(o, GLA_DV) * gg_ref[...] * (go * _sigmoid(go))


def _mixer_chain_kernel(proj_ref, glog_ref, cw_ref, cg_ref, gg_ref, cprev_ref, s0_ref, y_in_ref,
                        y_ref, cnew_ref, snew_ref, s_scr, tail_scr, *, chunk, nchunks):
    del y_in_ref
    blk = pl.program_id(1)
    tb = chunk * nchunks
    sub = min(SUB, chunk)

    @pl.when(blk == 0)
    def _():
        s_scr[...] = s0_ref[0]
        tail_scr[...] = cprev_ref[0]

    u = proj_ref[:, OFF_CC:OFF_CC + D_CONV] * proj_ref[:, OFF_CH:OFF_CH + D_CONV]
    prev = tail_scr[...]
    t8 = lax.broadcasted_iota(jnp.int32, (SUBLANES, D_CONV), 0)
    r1 = pltpu.roll(u, 1, 0)
    r2 = pltpu.roll(u, 2, 0)
    f1 = jnp.where(t8 < 1, pltpu.roll(prev, 1, 0), r1[0:SUBLANES])
    f2 = jnp.where(t8 < 2, pltpu.roll(prev, 2, 0), r2[0:SUBLANES])
    if tb > SUBLANES:
        u1 = jnp.concatenate([f1, r1[SUBLANES:]], axis=0)
        u2 = jnp.concatenate([f2, r2[SUBLANES:]], axis=0)
    else:
        u1, u2 = f1, f2
    tail_scr[...] = u[tb - SUBLANES:tb]
    _conv_branch(proj_ref[:, OFF_CB:OFF_CB + D_CONV], u, u1, u2, cw_ref, cg_ref, y_ref, slice(None))

    def chunk_step(single_ref, c):
        rows = slice(c * chunk, (c + 1) * chunk)
        q = proj_ref[rows, OFF_Q:OFF_Q + D_QK]
        k = proj_ref[rows, OFF_K:OFF_K + D_QK]
        g = glog_ref[rows, :]
        qb, kdec, dec_last, a_heads = _gla_rows(q, k, g, chunk, sub, single_ref)
        for hd in range(GLA_HEADS):
            kc = slice(hd * GLA_DK, (hd + 1) * GLA_DK)
            v = proj_ref[rows, OFF_V + hd * GLA_DV:OFF_V + (hd + 1) * GLA_DV].astype(BF16)
            go = proj_ref[rows, OFF_GO + hd * GLA_DV:OFF_GO + (hd + 1) * GLA_DV]
            s = s_scr[hd]
            o = jnp.dot(qb[:, kc].astype(BF16), s.astype(BF16), preferred_element_type=F32)
            o = o + jnp.dot(a_heads[hd].astype(BF16), v, preferred_element_type=F32)
            y_ref[rows, D_CONV + hd * GLA_DV:D_CONV + (hd + 1) * GLA_DV] = (
                _gla_out(o, go, gg_ref).astype(y_ref.dtype))
            kdec_t = kdec[:, kc].T.astype(BF16)
            dec_col = jnp.broadcast_to(dec_last[0:1, kc], (SUBLANES, GLA_DK)).T[:, 0:1]
            s_scr[hd] = s * dec_col + jnp.dot(kdec_t, v, preferred_element_type=F32)

    def run_chunks(single_ref):
        for c in range(nchunks):
            chunk_step(single_ref, c)

    if sub == chunk:
        run_chunks(False)
    else:
        decay_bound = -chunk * jnp.min(glog_ref[...])
        lax.cond(decay_bound <= MAX_FACTORED_DECAY,
                 functools.partial(run_chunks, True), functools.partial(run_chunks, False))

    @pl.when(blk == pl.num_programs(1) - 1)
    def _():
        snew_ref[0] = s_scr[...]
        cnew_ref[0] = tail_scr[...]


def _mixer_chain(proj, glog, cw, cg, gg, cprev, s0, y_buf, layer, *, row0, n_seq, seq_len, chunk, block):
    nblk = seq_len // block
    base = row0 // block
    assert row0 % block == 0 and seq_len % block == 0 and block % chunk == 0
    kern = functools.partial(_mixer_chain_kernel, chunk=chunk, nchunks=block // chunk)
    return pl.pallas_call(
        kern,
        grid=(n_seq, nblk),
        in_specs=[
            pl.BlockSpec((block, D_PROJ), lambda s, i: (base + s * nblk + i, 0)),
            pl.BlockSpec((block, D_QK), lambda s, i: (base + s * nblk + i, 0)),
            pl.BlockSpec((None, CONV_WIDTH, D_CONV), lambda s, i: (layer, 0, 0)),
            pl.BlockSpec((None, 1, D_CONV), lambda s, i: (layer, 0, 0)),
            pl.BlockSpec((None, 1, GLA_DV), lambda s, i: (layer, 0, 0)),
            pl.BlockSpec((1, SUBLANES, D_CONV), lambda s, i: (s, 0, 0)),
            pl.BlockSpec((1, GLA_HEADS, GLA_DK, GLA_DV), lambda s, i: (s, 0, 0, 0)),
            pl.BlockSpec(memory_space=pl.ANY),
        ],
        out_specs=[
            pl.BlockSpec((block, D_MODEL), lambda s, i: (base + s * nblk + i, 0)),
            pl.BlockSpec((1, SUBLANES, D_CONV), lambda s, i: (s, 0, 0)),
            pl.BlockSpec((1, GLA_HEADS, GLA_DK, GLA_DV), lambda s, i: (s, 0, 0, 0)),
        ],
        out_shape=[
            jax.ShapeDtypeStruct(y_buf.shape, y_buf.dtype),
            jax.ShapeDtypeStruct((n_seq, SUBLANES, D_CONV), F32),
            jax.ShapeDtypeStruct((n_seq, GLA_HEADS, GLA_DK, GLA_DV), F32),
        ],
        scratch_shapes=[
            pltpu.VMEM((GLA_HEADS, GLA_DK, GLA_DV), F32),
            pltpu.VMEM((SUBLANES, D_CONV), F32),
        ],
        input_output_aliases={7: 0},
        compiler_params=pltpu.CompilerParams(
            dimension_semantics=("arbitrary", "arbitrary"), vmem_limit_bytes=VMEM_LIMIT),
        name=f"mixer_chain{chunk}",
    )(proj, glog, cw, cg, gg, cprev, s0, y_buf)


def _mixer_step_kernel(proj_ref, glog_ref, cw_ref, cg_ref, gg_ref, cprev_ref, s0_ref,
                       y_in_ref, cnew_in_ref, snew_in_ref, y_ref, cnew_ref, snew_ref, *, nseq, seq_len):
    del y_in_ref, cnew_in_ref, snew_in_ref
    rows = nseq * seq_len
    assert seq_len == SUBLANES

    u = proj_ref[:, OFF_CC:OFF_CC + D_CONV] * proj_ref[:, OFF_CH:OFF_CH + D_CONV]
    u3 = u.reshape(nseq, seq_len, D_CONV)
    prev = cprev_ref[...]
    t8 = lax.broadcasted_iota(jnp.int32, (nseq, seq_len, D_CONV), 1)
    u1 = jnp.where(t8 < 1, pltpu.roll(prev, 1, 1), pltpu.roll(u3, 1, 1)).reshape(rows, D_CONV)
    u2 = jnp.where(t8 < 2, pltpu.roll(prev, 2, 1), pltpu.roll(u3, 2, 1)).reshape(rows, D_CONV)
    cnew_ref[...] = u3
    _conv_branch(proj_ref[:, OFF_CB:OFF_CB + D_CONV], u, u1, u2, cw_ref, cg_ref, y_ref, slice(None))

    q = proj_ref[:, OFF_Q:OFF_Q + D_QK]
    k = proj_ref[:, OFF_K:OFF_K + D_QK]
    g = glog_ref[...]
    qb, kdec, dec_last, a_heads = _gla_rows(q, k, g, seq_len, seq_len, False)

    qmask = (lax.broadcasted_iota(jnp.int32, (rows, nseq * GLA_DK), 0) // seq_len
             == lax.broadcasted_iota(jnp.int32, (rows, nseq * GLA_DK), 1) // GLA_DK)
    kmask = (lax.broadcasted_iota(jnp.int32, (nseq * GLA_DK, rows), 0) // GLA_DK
             == lax.broadcasted_iota(jnp.int32, (nseq * GLA_DK, rows), 1) // seq_len)
    for hd in range(GLA_HEADS):
        kc = slice(hd * GLA_DK, (hd + 1) * GLA_DK)
        v = proj_ref[:, OFF_V + hd * GLA_DV:OFF_V + (hd + 1) * GLA_DV].astype(BF16)
        go = proj_ref[:, OFF_GO + hd * GLA_DV:OFF_GO + (hd + 1) * GLA_DV]
        s = s0_ref[:, hd].reshape(nseq * GLA_DK, GLA_DV)
        q_blk = jnp.where(qmask, jnp.tile(qb[:, kc], (1, nseq)), 0.0).astype(BF16)
        o = jnp.dot(q_blk, s.astype(BF16), preferred_element_type=F32)
        o = o + jnp.dot(a_heads[hd].astype(BF16), v, preferred_element_type=F32)
        y_ref[:, D_CONV + hd * GLA_DV:D_CONV + (hd + 1) * GLA_DV] = (
            _gla_out(o, go, gg_ref).astype(y_ref.dtype))
        kdec_t = kdec[:, kc].T
        k_blk = jnp.where(kmask, jnp.tile(kdec_t, (nseq, 1)), 0.0).astype(BF16)
        upd = jnp.dot(k_blk, v, preferred_element_type=F32)
        dec_t = dec_last[:, kc].T
        for n in range(nseq):
            dec_col = dec_t[:, n * seq_len:n * seq_len + 1]
            snew_ref[n, hd] = (s[n * GLA_DK:(n + 1) * GLA_DK, :] * dec_col
                               + upd[n * GLA_DK:(n + 1) * GLA_DK, :])


def _mixer_step(proj, glog, cw, cg, gg, state_conv, state_gla, y_buf, new_conv, new_gla, layer, *,
                row0, n_seq, seq_len):
    rows = SAMPLE_SEQS * seq_len
    base = row0 // rows
    assert row0 % rows == 0 and n_seq % SAMPLE_SEQS == 0
    kern = functools.partial(_mixer_step_kernel, nseq=SAMPLE_SEQS, seq_len=seq_len)
    return pl.pallas_call(
        kern,
        grid=(n_seq // SAMPLE_SEQS,),
        in_specs=[
            pl.BlockSpec((rows, D_PROJ), lambda i: (base + i, 0)),
            pl.BlockSpec((rows, D_QK), lambda i: (base + i, 0)),
            pl.BlockSpec((None, CONV_WIDTH, D_CONV), lambda i: (layer, 0, 0)),
            pl.BlockSpec((None, 1, D_CONV), lambda i: (layer, 0, 0)),
            pl.BlockSpec((None, 1, GLA_DV), lambda i: (layer, 0, 0)),
            pl.BlockSpec((None, SAMPLE_SEQS, SUBLANES, D_CONV), lambda i: (layer, i, 0, 0)),
            pl.BlockSpec((None, SAMPLE_SEQS, GLA_HEADS, GLA_DK, GLA_DV), lambda i: (layer, i, 0, 0, 0)),
        ] + [pl.BlockSpec(memory_space=pl.ANY)] * 3,
        out_specs=[
            pl.BlockSpec((rows, D_MODEL), lambda i: (base + i, 0)),
            pl.BlockSpec((None, SAMPLE_SEQS, SUBLANES, D_CONV), lambda i: (layer, i, 0, 0)),
            pl.BlockSpec((None, SAMPLE_SEQS, GLA_HEADS, GLA_DK, GLA_DV), lambda i: (layer, i, 0, 0, 0)),
        ],
        out_shape=[
            jax.ShapeDtypeStruct(y_buf.shape, y_buf.dtype),
            jax.ShapeDtypeStruct(state_conv.shape, F32),
            jax.ShapeDtypeStruct(state_gla.shape, F32),
        ],
        input_output_aliases={7: 0, 8: 1, 9: 2},
        compiler_params=pltpu.CompilerParams(
            dimension_semantics=("arbitrary",), vmem_limit_bytes=VMEM_LIMIT),
        name="mixer_step",
    )(proj, glog, cw, cg, gg, state_conv, state_gla, y_buf, new_conv, new_gla)


def _mixout_kernel(h_ref, y_ref, w_ref, o_ref):
    o_ref[...] = h_ref[...] + jnp.dot(y_ref[...], w_ref[...].astype(BF16), preferred_element_type=F32)


def _mixout(h, y, w_out, layer):
    n_tok = h.shape[0]
    return pl.pallas_call(
        _mixout_kernel,
        grid=(n_tok // TM_OUT,),
        in_specs=[
            pl.BlockSpec((TM_OUT, D_MODEL), lambda i: (i, 0)),
            pl.BlockSpec((TM_OUT, D_MODEL), lambda i: (i, 0)),
            pl.BlockSpec((None, D_MODEL, D_MODEL), lambda i: (layer, 0, 0), pipeline_mode=pl.Buffered(1)),
        ],
        out_specs=pl.BlockSpec((TM_OUT, D_MODEL), lambda i: (i, 0)),
        out_shape=jax.ShapeDtypeStruct((n_tok, D_MODEL), F32),
        compiler_params=pltpu.CompilerParams(
            dimension_semantics=("arbitrary",), vmem_limit_bytes=VMEM_LIMIT),
        name="mixout",
    )(h, y, w_out)


def _final_norm_kernel(h_ref, g_ref, o_ref):
    h = h_ref[...]
    o_ref[...] = h * _rms_scale(h, D_MODEL) * g_ref[...]


def _final_norm(h, gain, *, row0, n_rows, tile):
    base = row0 // tile
    assert row0 % tile == 0 and n_rows % tile == 0
    return pl.pallas_call(
        _final_norm_kernel,
        grid=(n_rows // tile,),
        in_specs=[
            pl.BlockSpec((tile, D_MODEL), lambda i: (base + i, 0)),
            pl.BlockSpec((1, D_MODEL), lambda i: (0, 0)),
        ],
        out_specs=pl.BlockSpec((tile, D_MODEL), lambda i: (i, 0)),
        out_shape=jax.ShapeDtypeStruct((n_rows, D_MODEL), F32),
        compiler_params=pltpu.CompilerParams(
            dimension_semantics=("arbitrary",), vmem_limit_bytes=VMEM_LIMIT),
        name="final_norm",
    )(h, gain)


def _pad_conv_state(c):
    return jnp.pad(c, ((0, 0),) * (c.ndim - 2) + ((SUBLANES - (CONV_WIDTH - 1), 0), (0, 0)))


def kernel(x_prompt, x_sample, state_conv, state_gla, meta_tokens, norm_ffn1, w_ffn1_gu, w_ffn1_down,
           norm_mix, w_mix_in, conv_w, conv_norm, gla_fgate_w2, gla_fgate_b, gla_out_norm, w_mix_out,
           norm_ffn2, w_ffn2_gu, w_ffn2_down, norm_final):
    assert x_prompt.shape == (BATCH, SEQ, D_MODEL) and x_sample.shape == (DEC_BATCH, DEC_SEQ, D_MODEL)
    h = jnp.concatenate([
        x_prompt.reshape(N_PROMPT, D_MODEL),
        x_sample.reshape(N_SAMPLE, D_MODEL),
        jnp.tile(meta_tokens.astype(x_prompt.dtype), (BATCH, 1)),
    ], axis=0)

    wgu1, wd1, wgu2, wd2, w_out = w_ffn1_gu, w_ffn1_down, w_ffn2_gu, w_ffn2_down, w_mix_out
    w_in = jnp.swapaxes(w_mix_in, 1, 2)
    fw2 = jnp.pad(gla_fgate_w2, ((0, 0), (0, LANES - GLA_GATE_RANK), (0, 0))).astype(BF16)
    n1, nm, n2 = (g.reshape(DEPTH, 1, D_MODEL) for g in (norm_ffn1, norm_mix, norm_ffn2))
    fb = gla_fgate_b.reshape(DEPTH, 1, D_QK)
    mix_w = (conv_w, conv_norm.reshape(DEPTH, 1, D_CONV), gla_out_norm.reshape(DEPTH, 1, GLA_DV))
    conv_s_in = _pad_conv_state(state_conv)

    zero_conv = jnp.zeros((BATCH, SUBLANES, D_CONV), F32)
    zero_gla = jnp.zeros((BATCH, GLA_HEADS, GLA_DK, GLA_DV), F32)
    y = jnp.zeros((N_TOK, D_MODEL), BF16)
    conv_s = jnp.zeros(conv_s_in.shape, F32)
    gla_s = jnp.zeros(state_gla.shape, F32)
    conv_p, gla_p = [], []
    for l in range(DEPTH):
        h = _ffn(h, n1, wgu1, wd1, l)
        proj, glog = _mixin(h, nm, w_in, fw2, fb, l)
        y, c_m, s_m = _mixer_chain(proj, glog, *mix_w, zero_conv, zero_gla, y, l, row0=ROW_META,
                                   n_seq=BATCH, seq_len=N_META, chunk=N_META, block=N_META)
        y, c_p, s_p = _mixer_chain(proj, glog, *mix_w, c_m, s_m, y, l, row0=0,
                                   n_seq=BATCH, seq_len=SEQ, chunk=GLA_CHUNK, block=MIX_BLOCK)
        y, conv_s, gla_s = _mixer_step(proj, glog, *mix_w, conv_s_in, state_gla, y, conv_s, gla_s, l,
                                       row0=ROW_SAMPLE, n_seq=DEC_BATCH, seq_len=DEC_SEQ)
        h = _mixout(h, y, w_out, l)
        h = _ffn(h, n2, wgu2, wd2, l)
        conv_p.append(c_p[:, SUBLANES - (CONV_WIDTH - 1):])
        gla_p.append(s_p)

    y_prompt = _final_norm(h, norm_final[None], row0=0, n_rows=N_PROMPT, tile=512)
    y_sample = _final_norm(h, norm_final[None], row0=ROW_SAMPLE, n_rows=N_SAMPLE, tile=512)
    return (y_prompt.reshape(BATCH, SEQ, D_MODEL), y_sample.reshape(DEC_BATCH, DEC_SEQ, D_MODEL),
            jnp.stack(gla_p), jnp.stack(conv_p), gla_s, conv_s[:, :, SUBLANES - (CONV_WIDTH - 1):])
```

```python
import functools

import jax
import jax.numpy as jnp
from jax import lax
from jax.experimental import pallas as pl
from jax.experimental.pallas import tpu as pltpu

D_MODEL = 2048
BATCH = 4
SEQ = 2048
DEPTH = 4
DEC_BATCH = 128
DEC_SEQ = 8
N_META = 16
D_CONV = 1024
CONV_WIDTH = 3
CONV_GROUPS = 8
CONV_GROUP_WIDTH = D_CONV // CONV_GROUPS
GLA_HEADS = 4
GLA_DV = 256
GLA_DK = 128
GLA_GATE_RANK = 16
GLA_GATE_TAU = 16.0
GLA_CHUNK = 64
D_FF = 5632
EPS = 1e-6
D_QK = GLA_HEADS * GLA_DK
D_V = GLA_HEADS * GLA_DV
D_PROJ = 3 * D_CONV + 2 * D_QK + 2 * D_V

OFF_CB, OFF_CC, OFF_CH = 0, D_CONV, 2 * D_CONV
OFF_Q = 3 * D_CONV
OFF_K = OFF_Q + D_QK
OFF_V = OFF_K + D_QK
OFF_GO = OFF_V + D_V

N_PROMPT = BATCH * SEQ
N_SAMPLE = DEC_BATCH * DEC_SEQ
N_METAROWS = BATCH * N_META
ROW_SAMPLE = N_PROMPT
ROW_META = N_PROMPT + N_SAMPLE
N_TOK = ROW_META + N_METAROWS

SUBLANES = 8
LANES = 128
VMEM_LIMIT = 63 * 1024 * 1024

TM = 928
TF = 512
TN_MIX = 1024
TM_OUT = 464
SUB = 16
MAX_FACTORED_DECAY = 40.0
MIX_BLOCK = 256
SAMPLE_SEQS = 8

F32 = jnp.float32
BF16 = jnp.bfloat16


def _rms_scale(x, width):
    return lax.rsqrt(jnp.sum(x * x, axis=-1, keepdims=True) * (1.0 / width) + EPS)


def _sigmoid(x):
    return 1.0 / (1.0 + jnp.exp(-x))


def _ffn_kernel(h_ref, g_ref, wg_ref, wu_ref, wd_ref, o_ref, xn_ref):
    j = pl.program_id(1)

    @pl.when(j == 0)
    def _():
        h = h_ref[...]
        xn_ref[...] = (h * _rms_scale(h, D_MODEL) * g_ref[...]).astype(BF16)

    xn = xn_ref[...]
    a = jnp.dot(xn, wg_ref[...].astype(BF16), preferred_element_type=F32)
    b = jnp.dot(xn, wu_ref[...].astype(BF16), preferred_element_type=F32)
    act = (a * _sigmoid(a) * (b * 0.5)).astype(BF16)
    acc = jnp.where(j == 0, h_ref[...], o_ref[...])
    o_ref[...] = acc + jnp.dot(act, wd_ref[...].astype(BF16), preferred_element_type=F32)


def _ffn(h, gain, w_gu, w_down, layer):
    n_tok = h.shape[0]
    nf = D_FF // TF
    return pl.pallas_call(
        _ffn_kernel,
        grid=(n_tok // TM, nf),
        in_specs=[
            pl.BlockSpec((TM, D_MODEL), lambda i, j: (i, 0)),
            pl.BlockSpec((None, 1, D_MODEL), lambda i, j: (layer, 0, 0)),
            pl.BlockSpec((None, D_MODEL, TF), lambda i, j: (layer, 0, j)),
            pl.BlockSpec((None, D_MODEL, TF), lambda i, j: (layer, 0, j + nf)),
            pl.BlockSpec((None, TF, D_MODEL), lambda i, j: (layer, j, 0)),
        ],
        out_specs=pl.BlockSpec((TM, D_MODEL), lambda i, j: (i, 0)),
        out_shape=jax.ShapeDtypeStruct((n_tok, D_MODEL), F32),
        scratch_shapes=[pltpu.VMEM((TM, D_MODEL), BF16)],
        compiler_params=pltpu.CompilerParams(
            dimension_semantics=("arbitrary", "arbitrary"), vmem_limit_bytes=VMEM_LIMIT),
        name="ffn",
    )(h, gain, w_gu, w_gu, w_down)


_NT = (((1,), (1,)), ((), ()))


def _mixin_kernel(h_ref, g_ref, wt_ref, wflt_ref, fw2_ref, fb_ref, proj_ref, glog_ref, xn_ref):
    j = pl.program_id(1)
    half = pl.program_id(2)

    @pl.when(j == 0)
    def _():
        h = h_ref[...]
        xn = (h * _rms_scale(h, D_MODEL) * g_ref[...]).astype(BF16)
        xn_ref[half] = xn
        fl = lax.dot_general(xn, wflt_ref[...].astype(BF16), _NT, preferred_element_type=F32)
        fl = jnp.where(lax.broadcasted_iota(jnp.int32, fl.shape, 1) < GLA_GATE_RANK, fl, 0.0)
        z = jnp.dot(fl.astype(BF16), fw2_ref[...], preferred_element_type=F32) + fb_ref[...]
        glog_ref[...] = (jnp.minimum(z, 0.0) - jnp.log1p(jnp.exp(-jnp.abs(z)))) * (1.0 / GLA_GATE_TAU)

    proj_ref[...] = lax.dot_general(xn_ref[half], wt_ref[...].astype(BF16), _NT, preferred_element_type=F32)


def _mixin(h, gain, w_in_t, fw2, fb, layer):
    n_tok = h.shape[0]
    assert n_tok % (2 * TM) == 0

    def tile(p, j, half):
        return jnp.where(j == 0, 2 * p + half, 2 * p + 1)

    return pl.pallas_call(
        _mixin_kernel,
        grid=(n_tok // (2 * TM), D_PROJ // TN_MIX, 2),
        in_specs=[
            pl.BlockSpec((TM, D_MODEL), lambda p, j, half: (tile(p, j, half), 0)),
            pl.BlockSpec((None, 1, D_MODEL), lambda p, j, half: (layer, 0, 0)),
            pl.BlockSpec((None, TN_MIX, D_MODEL), lambda p, j, half: (layer, j, 0)),
            pl.BlockSpec((None, LANES, D_MODEL), lambda p, j, half: (layer, D_PROJ // LANES, 0)),
            pl.BlockSpec((None, LANES, D_QK), lambda p, j, half: (layer, 0, 0)),
            pl.BlockSpec((None, 1, D_QK), lambda p, j, half: (layer, 0, 0)),
        ],
        out_specs=[
            pl.BlockSpec((TM, TN_MIX), lambda p, j, half: (2 * p + half, j)),
            pl.BlockSpec((TM, D_QK), lambda p, j, half: (tile(p, j, half), 0)),
        ],
        out_shape=[
            jax.ShapeDtypeStruct((n_tok, D_PROJ), F32),
            jax.ShapeDtypeStruct((n_tok, D_QK), F32),
        ],
        scratch_shapes=[pltpu.VMEM((2, TM, D_MODEL), BF16)],
        compiler_params=pltpu.CompilerParams(
            dimension_semantics=("arbitrary", "arbitrary", "arbitrary"), vmem_limit_bytes=VMEM_LIMIT),
        name="mixin",
    )(h, gain, w_in_t, w_in_t, fw2, fb)


def _block_rows(x, block, j):
    rows, width = x.shape
    if rows == block:
        return jnp.broadcast_to(x[j:j + 1, :], (rows, width))
    x3 = x.reshape(rows // block, block, width)
    return jnp.broadcast_to(x3[:, j:j + 1, :], (rows // block, block, width)).reshape(rows, width)


def _conv_branch(cb, u, u1, u2, cw_ref, cg_ref, y_ref, rows):
    conv = cw_ref[0:1, :] * u2 + cw_ref[1:2, :] * u1 + cw_ref[2:3, :] * u
    z = cb * conv
    for grp in range(CONV_GROUPS):
        cols = slice(grp * CONV_GROUP_WIDTH, (grp + 1) * CONV_GROUP_WIDTH)
        zg = z[:, cols]
        y_ref[rows, cols] = (zg * _rms_scale(zg, CONV_GROUP_WIDTH) * cg_ref[:, cols]).astype(y_ref.dtype)


def _segment_mask(rows, seg):
    ri = lax.broadcasted_iota(jnp.int32, (rows, rows), 0)
    ci = lax.broadcasted_iota(jnp.int32, (rows, rows), 1)
    causal = ci <= ri
    if rows != seg:
        causal = causal & ((ri // seg) == (ci // seg))
    return ri, ci, causal


def _cum_gates(g, seg):
    _, _, causal = _segment_mask(g.shape[0], seg)
    mcum = jnp.where(causal, 1.0, 0.0).astype(F32)
    return jnp.dot(mcum, g, preferred_element_type=F32, precision=lax.Precision.HIGHEST)


def _gla_rows(q, k, b, seg, sub, single_ref):
    rows = q.shape[0]
    ri, ci, causal = _segment_mask(rows, seg)
    qs = q * (GLA_DK ** -0.5)
    b_last = _block_rows(b, seg, seg - 1)
    qb = qs * jnp.exp(b)
    dec_last = jnp.exp(b_last)

    if single_ref:
        kinv = k * jnp.exp(-b)
        kdec = kinv * dec_last
        qb16, kinv16 = qb.astype(BF16), kinv.astype(BF16)
        a_heads = []
        for hd in range(GLA_HEADS):
            cols = slice(hd * GLA_DK, (hd + 1) * GLA_DK)
            a_h = lax.dot_general(qb16[:, cols], kinv16[:, cols], _NT, preferred_element_type=F32)
            a_heads.append(jnp.where(causal, a_h, 0.0))
        return qb, kdec, dec_last, a_heads

    blk0 = (ri // sub) * sub
    dcol = jnp.where(causal & (ci >= blk0), ci - blk0, -1)
    a_heads = [jnp.zeros((rows, rows), F32) for _ in range(GLA_HEADS)]
    for j in range(sub):
        kb = _block_rows(k, sub, j)
        bb = _block_rows(b, sub, j)
        p = qs * kb * jnp.exp(jnp.minimum(b - bb, 0.0))
        hit = dcol == j
        for hd in range(GLA_HEADS):
            r = jnp.sum(p[:, hd * GLA_DK:(hd + 1) * GLA_DK], axis=-1, keepdims=True)
            a_heads[hd] = jnp.where(hit, r, a_heads[hd])

    nsub = seg // sub
    if nsub > 1:
        assert rows == seg
        off = [[jnp.zeros((sub, rows), F32)] for _ in range(GLA_HEADS)]
        ci_s = lax.broadcasted_iota(jnp.int32, (sub, rows), 1)
        for i in range(1, nsub):
            lo = i * sub
            bs = b[lo - 1:lo, :]
            qd = (qs[lo:lo + sub, :] * jnp.exp(b[lo:lo + sub, :] - bs)).astype(BF16)
            kd = (k * jnp.exp(jnp.minimum(bs - b, 0.0))).astype(BF16)
            for hd in range(GLA_HEADS):
                cols = slice(hd * GLA_DK, (hd + 1) * GLA_DK)
                a_i = lax.dot_general(qd[:, cols], kd[:, cols], _NT, preferred_element_type=F32)
                off[hd].append(jnp.where(ci_s < lo, a_i, 0.0))
        a_heads = [a_heads[hd] + jnp.concatenate(off[hd], axis=0) for hd in range(GLA_HEADS)]

    kdec = k * jnp.exp(b_last - b)
    return qb, kdec, dec_last, a_heads


def _gla_out(o, go, gg_ref):
    return o * _rms_scale(o, GLA_DV) * gg_ref[...] * (go * _sigmoid(go))


def _mixer_chain_kernel(proj_ref, glog_ref, cw_ref, cg_ref, gg_ref, cprev_ref, s0_ref, y_in_ref,
                        y_ref, cnew_ref, snew_ref, s_scr, tail_scr, *, chunk, nchunks):
    del y_in_ref
    blk = pl.program_id(1)
    tb = chunk * nchunks
    sub = min(SUB, chunk)

    @pl.when(blk == 0)
    def _():
        s_scr[...] = s0_ref[0]
        tail_scr[...] = cprev_ref[0]

    u = proj_ref[:, OFF_CC:OFF_CC + D_CONV] * proj_ref[:, OFF_CH:OFF_CH + D_CONV]
    prev = tail_scr[...]
    t8 = lax.broadcasted_iota(jnp.int32, (SUBLANES, D_CONV), 0)
    r1 = pltpu.roll(u, 1, 0)
    r2 = pltpu.roll(u, 2, 0)
    f1 = jnp.where(t8 < 1, pltpu.roll(prev, 1, 0), r1[0:SUBLANES])
    f2 = jnp.where(t8 < 2, pltpu.roll(prev, 2, 0), r2[0:SUBLANES])
    if tb > SUBLANES:
        u1 = jnp.concatenate([f1, r1[SUBLANES:]], axis=0)
        u2 = jnp.concatenate([f2, r2[SUBLANES:]], axis=0)
    else:
        u1, u2 = f1, f2
    tail_scr[...] = u[tb - SUBLANES:tb]
    _conv_branch(proj_ref[:, OFF_CB:OFF_CB + D_CONV], u, u1, u2, cw_ref, cg_ref, y_ref, slice(None))

    b_blk = _cum_gates(glog_ref[...], chunk)

    def chunk_step(single_ref, c):
        rows = slice(c * chunk, (c + 1) * chunk)
        q = proj_ref[rows, OFF_Q:OFF_Q + D_QK]
        k = proj_ref[rows, OFF_K:OFF_K + D_QK]
        qb, kdec, dec_last, a_heads = _gla_rows(q, k, b_blk[rows, :], chunk, sub, single_ref)
        for hd in range(GLA_HEADS):
            kc = slice(hd * GLA_DK, (hd + 1) * GLA_DK)
            v = proj_ref[rows, OFF_V + hd * GLA_DV:OFF_V + (hd + 1) * GLA_DV].astype(BF16)
            go = proj_ref[rows, OFF_GO + hd * GLA_DV:OFF_GO + (hd + 1) * GLA_DV]
            s = s_scr[hd]
            lhs = jnp.concatenate([qb[:, kc], a_heads[hd]], axis=1).astype(BF16)
            rhs = jnp.concatenate([s.astype(BF16), v], axis=0)
            o = jnp.dot(lhs, rhs, preferred_element_type=F32)
            y_ref[rows, D_CONV + hd * GLA_DV:D_CONV + (hd + 1) * GLA_DV] = (
                _gla_out(o, go, gg_ref).astype(y_ref.dtype))
            kdec_t = kdec[:, kc].T.astype(BF16)
            dec_col = jnp.broadcast_to(dec_last[0:1, kc], (SUBLANES, GLA_DK)).T[:, 0:1]
            s_scr[hd] = s * dec_col + jnp.dot(kdec_t, v, preferred_element_type=F32)

    def run_chunks(single_ref):
        for c in range(nchunks):
            chunk_step(single_ref, c)

    if sub == chunk:
        run_chunks(False)
    else:
        decay_bound = -chunk * jnp.min(glog_ref[...])
        lax.cond(decay_bound <= MAX_FACTORED_DECAY,
                 functools.partial(run_chunks, True), functools.partial(run_chunks, False))

    @pl.when(blk == pl.num_programs(1) - 1)
    def _():
        snew_ref[0] = s_scr[...]
        cnew_ref[0] = tail_scr[...]


def _mixer_chain(proj, glog, cw, cg, gg, cprev, s0, y_buf, layer, *, row0, n_seq, seq_len, chunk, block):
    nblk = seq_len // block
    base = row0 // block
    assert row0 % block == 0 and seq_len % block == 0 and block % chunk == 0
    kern = functools.partial(_mixer_chain_kernel, chunk=chunk, nchunks=block // chunk)
    return pl.pallas_call(
        kern,
        grid=(n_seq, nblk),
        in_specs=[
            pl.BlockSpec((block, D_PROJ), lambda s, i: (base + s * nblk + i, 0)),
            pl.BlockSpec((block, D_QK), lambda s, i: (base + s * nblk + i, 0)),
            pl.BlockSpec((None, CONV_WIDTH, D_CONV), lambda s, i: (layer, 0, 0)),
            pl.BlockSpec((None, 1, D_CONV), lambda s, i: (layer, 0, 0)),
            pl.BlockSpec((None, 1, GLA_DV), lambda s, i: (layer, 0, 0)),
            pl.BlockSpec((1, SUBLANES, D_CONV), lambda s, i: (s, 0, 0)),
            pl.BlockSpec((1, GLA_HEADS, GLA_DK, GLA_DV), lambda s, i: (s, 0, 0, 0)),
            pl.BlockSpec(memory_space=pl.ANY),
        ],
        out_specs=[
            pl.BlockSpec((block, D_MODEL), lambda s, i: (base + s * nblk + i, 0)),
            pl.BlockSpec((1, SUBLANES, D_CONV), lambda s, i: (s, 0, 0)),
            pl.BlockSpec((1, GLA_HEADS, GLA_DK, GLA_DV), lambda s, i: (s, 0, 0, 0)),
        ],
        out_shape=[
            jax.ShapeDtypeStruct(y_buf.shape, y_buf.dtype),
            jax.ShapeDtypeStruct((n_seq, SUBLANES, D_CONV), F32),
            jax.ShapeDtypeStruct((n_seq, GLA_HEADS, GLA_DK, GLA_DV), F32),
        ],
        scratch_shapes=[
            pltpu.VMEM((GLA_HEADS, GLA_DK, GLA_DV), F32),
            pltpu.VMEM((SUBLANES, D_CONV), F32),
        ],
        input_output_aliases={7: 0},
        compiler_params=pltpu.CompilerParams(
            dimension_semantics=("arbitrary", "arbitrary"), vmem_limit_bytes=VMEM_LIMIT),
        name=f"mixer_chain{chunk}",
    )(proj, glog, cw, cg, gg, cprev, s0, y_buf)


def _mixer_step_kernel(proj_ref, glog_ref, cw_ref, cg_ref, gg_ref, cprev_ref, s0_ref,
                       y_in_ref, cnew_in_ref, snew_in_ref, y_ref, cnew_ref, snew_ref, *, nseq, seq_len):
    del y_in_ref, cnew_in_ref, snew_in_ref
    rows = nseq * seq_len
    assert seq_len == SUBLANES

    u = proj_ref[:, OFF_CC:OFF_CC + D_CONV] * proj_ref[:, OFF_CH:OFF_CH + D_CONV]
    u3 = u.reshape(nseq, seq_len, D_CONV)
    prev = cprev_ref[...]
    t8 = lax.broadcasted_iota(jnp.int32, (nseq, seq_len, D_CONV), 1)
    u1 = jnp.where(t8 < 1, pltpu.roll(prev, 1, 1), pltpu.roll(u3, 1, 1)).reshape(rows, D_CONV)
    u2 = jnp.where(t8 < 2, pltpu.roll(prev, 2, 1), pltpu.roll(u3, 2, 1)).reshape(rows, D_CONV)
    cnew_ref[...] = u3
    _conv_branch(proj_ref[:, OFF_CB:OFF_CB + D_CONV], u, u1, u2, cw_ref, cg_ref, y_ref, slice(None))

    q = proj_ref[:, OFF_Q:OFF_Q + D_QK]
    k = proj_ref[:, OFF_K:OFF_K + D_QK]
    b = _cum_gates(glog_ref[...], seq_len)
    qb, kdec, dec_last, a_heads = _gla_rows(q, k, b, seq_len, seq_len, False)

    qmask = (lax.broadcasted_iota(jnp.int32, (rows, nseq * GLA_DK), 0) // seq_len
             == lax.broadcasted_iota(jnp.int32, (rows, nseq * GLA_DK), 1) // GLA_DK)
    kmask = (lax.broadcasted_iota(jnp.int32, (nseq * GLA_DK, rows), 0) // GLA_DK
             == lax.broadcasted_iota(jnp.int32, (nseq * GLA_DK, rows), 1) // seq_len)
    for hd in range(GLA_HEADS):
        kc = slice(hd * GLA_DK, (hd + 1) * GLA_DK)
        v = proj_ref[:, OFF_V + hd * GLA_DV:OFF_V + (hd + 1) * GLA_DV].astype(BF16)
        go = proj_ref[:, OFF_GO + hd * GLA_DV:OFF_GO + (hd + 1) * GLA_DV]
        s = s0_ref[:, hd].reshape(nseq * GLA_DK, GLA_DV)
        q_blk = jnp.where(qmask, jnp.tile(qb[:, kc], (1, nseq)), 0.0).astype(BF16)
        o = jnp.dot(q_blk, s.astype(BF16), preferred_element_type=F32)
        o = o + jnp.dot(a_heads[hd].astype(BF16), v, preferred_element_type=F32)
        y_ref[:, D_CONV + hd * GLA_DV:D_CONV + (hd + 1) * GLA_DV] = (
            _gla_out(o, go, gg_ref).astype(y_ref.dtype))
        kdec_t = kdec[:, kc].T
        k_blk = jnp.where(kmask, jnp.tile(kdec_t, (nseq, 1)), 0.0).astype(BF16)
        upd = jnp.dot(k_blk, v, preferred_element_type=F32)
        dec_t = dec_last[:, kc].T
        for n in range(nseq):
            dec_col = dec_t[:, n * seq_len:n * seq_len + 1]
            snew_ref[n, hd] = (s[n * GLA_DK:(n + 1) * GLA_DK, :] * dec_col
                               + upd[n * GLA_DK:(n + 1) * GLA_DK, :])


def _mixer_step(proj, glog, cw, cg, gg, state_conv, state_gla, y_buf, new_conv, new_gla, layer, *,
                row0, n_seq, seq_len):
    rows = SAMPLE_SEQS * seq_len
    base = row0 // rows
    assert row0 % rows == 0 and n_seq % SAMPLE_SEQS == 0
    kern = functools.partial(_mixer_step_kernel, nseq=SAMPLE_SEQS, seq_len=seq_len)
    return pl.pallas_call(
        kern,
        grid=(n_seq // SAMPLE_SEQS,),
        in_specs=[
            pl.BlockSpec((rows, D_PROJ), lambda i: (base + i, 0)),
            pl.BlockSpec((rows, D_QK), lambda i: (base + i, 0)),
            pl.BlockSpec((None, CONV_WIDTH, D_CONV), lambda i: (layer, 0, 0)),
            pl.BlockSpec((None, 1, D_CONV), lambda i: (layer, 0, 0)),
            pl.BlockSpec((None, 1, GLA_DV), lambda i: (layer, 0, 0)),
            pl.BlockSpec((None, SAMPLE_SEQS, SUBLANES, D_CONV), lambda i: (layer, i, 0, 0)),
            pl.BlockSpec((None, SAMPLE_SEQS, GLA_HEADS, GLA_DK, GLA_DV), lambda i: (layer, i, 0, 0, 0)),
        ] + [pl.BlockSpec(memory_space=pl.ANY)] * 3,
        out_specs=[
            pl.BlockSpec((rows, D_MODEL), lambda i: (base + i, 0)),
            pl.BlockSpec((None, SAMPLE_SEQS, SUBLANES, D_CONV), lambda i: (layer, i, 0, 0)),
            pl.BlockSpec((None, SAMPLE_SEQS, GLA_HEADS, GLA_DK, GLA_DV), lambda i: (layer, i, 0, 0, 0)),
        ],
        out_shape=[
            jax.ShapeDtypeStruct(y_buf.shape, y_buf.dtype),
            jax.ShapeDtypeStruct(state_conv.shape, F32),
            jax.ShapeDtypeStruct(state_gla.shape, F32),
        ],
        input_output_aliases={7: 0, 8: 1, 9: 2},
        compiler_params=pltpu.CompilerParams(
            dimension_semantics=("arbitrary",), vmem_limit_bytes=VMEM_LIMIT),
        name="mixer_step",
    )(proj, glog, cw, cg, gg, state_conv, state_gla, y_buf, new_conv, new_gla)


def _mixout_kernel(h_ref, y_ref, w_ref, o_ref):
    o_ref[...] = h_ref[...] + jnp.dot(y_ref[...], w_ref[...].astype(BF16), preferred_element_type=F32)


def _mixout(h, y, w_out, layer):
    n_tok = h.shape[0]
    return pl.pallas_call(
        _mixout_kernel,
        grid=(n_tok // TM_OUT,),
        in_specs=[
            pl.BlockSpec((TM_OUT, D_MODEL), lambda i: (i, 0)),
            pl.BlockSpec((TM_OUT, D_MODEL), lambda i: (i, 0)),
            pl.BlockSpec((None, D_MODEL, D_MODEL), lambda i: (layer, 0, 0), pipeline_mode=pl.Buffered(1)),
        ],
        out_specs=pl.BlockSpec((TM_OUT, D_MODEL), lambda i: (i, 0)),
        out_shape=jax.ShapeDtypeStruct((n_tok, D_MODEL), F32),
        compiler_params=pltpu.CompilerParams(
            dimension_semantics=("arbitrary",), vmem_limit_bytes=VMEM_LIMIT),
        name="mixout",
    )(h, y, w_out)


def _final_norm_kernel(h_ref, g_ref, o_ref):
    h = h_ref[...]
    o_ref[...] = h * _rms_scale(h, D_MODEL) * g_ref[...]


def _final_norm(h, gain, *, row0, n_rows, tile):
    base = row0 // tile
    assert row0 % tile == 0 and n_rows % tile == 0
    return pl.pallas_call(
        _final_norm_kernel,
        grid=(n_rows // tile,),
        in_specs=[
            pl.BlockSpec((tile, D_MODEL), lambda i: (base + i, 0)),
            pl.BlockSpec((1, D_MODEL), lambda i: (0, 0)),
        ],
        out_specs=pl.BlockSpec((tile, D_MODEL), lambda i: (i, 0)),
        out_shape=jax.ShapeDtypeStruct((n_rows, D_MODEL), F32),
        compiler_params=pltpu.CompilerParams(
            dimension_semantics=("arbitrary",), vmem_limit_bytes=VMEM_LIMIT),
        name="final_norm",
    )(h, gain)


def _pad_conv_state(c):
    return jnp.pad(c, ((0, 0),) * (c.ndim - 2) + ((SUBLANES - (CONV_WIDTH - 1), 0), (0, 0)))


def kernel(x_prompt, x_sample, state_conv, state_gla, meta_tokens, norm_ffn1, w_ffn1_gu, w_ffn1_down,
           norm_mix, w_mix_in, conv_w, conv_norm, gla_fgate_w2, gla_fgate_b, gla_out_norm, w_mix_out,
           norm_ffn2, w_ffn2_gu, w_ffn2_down, norm_final):
    assert x_prompt.shape == (BATCH, SEQ, D_MODEL) and x_sample.shape == (DEC_BATCH, DEC_SEQ, D_MODEL)
    h = jnp.concatenate([
        x_prompt.reshape(N_PROMPT, D_MODEL),
        x_sample.reshape(N_SAMPLE, D_MODEL),
        jnp.tile(meta_tokens.astype(x_prompt.dtype), (BATCH, 1)),
    ], axis=0)

    wgu1, wd1, wgu2, wd2, w_out = w_ffn1_gu, w_ffn1_down, w_ffn2_gu, w_ffn2_down, w_mix_out
    w_in = jnp.swapaxes(w_mix_in, 1, 2)
    fw2 = jnp.pad(gla_fgate_w2, ((0, 0), (0, LANES - GLA_GATE_RANK), (0, 0))).astype(BF16)
    n1, nm, n2 = (g.reshape(DEPTH, 1, D_MODEL) for g in (norm_ffn1, norm_mix, norm_ffn2))
    fb = gla_fgate_b.reshape(DEPTH, 1, D_QK)
    mix_w = (conv_w, conv_norm.reshape(DEPTH, 1, D_CONV), gla_out_norm.reshape(DEPTH, 1, GLA_DV))
    conv_s_in = _pad_conv_state(state_conv)

    zero_conv = jnp.zeros((BATCH, SUBLANES, D_CONV), F32)
    zero_gla = jnp.zeros((BATCH, GLA_HEADS, GLA_DK, GLA_DV), F32)
    y = jnp.zeros((N_TOK, D_MODEL), BF16)
    conv_s = jnp.zeros(conv_s_in.shape, F32)
    gla_s = jnp.zeros(state_gla.shape, F32)
    conv_p, gla_p = [], []
    for l in range(DEPTH):
        h = _ffn(h, n1, wgu1, wd1, l)
        proj, glog = _mixin(h, nm, w_in, fw2, fb, l)
        y, c_m, s_m = _mixer_chain(proj, glog, *mix_w, zero_conv, zero_gla, y, l, row0=ROW_META,
                                   n_seq=BATCH, seq_len=N_META, chunk=N_META, block=N_META)
        y, c_p, s_p = _mixer_chain(proj, glog, *mix_w, c_m, s_m, y, l, row0=0,
                                   n_seq=BATCH, seq_len=SEQ, chunk=GLA_CHUNK, block=MIX_BLOCK)
        y, conv_s, gla_s = _mixer_step(proj, glog, *mix_w, conv_s_in, state_gla, y, conv_s, gla_s, l,
                                       row0=ROW_SAMPLE, n_seq=DEC_BATCH, seq_len=DEC_SEQ)
        h = _mixout(h, y, w_out, l)
        h = _ffn(h, n2, wgu2, wd2, l)
        conv_p.append(c_p[:, SUBLANES - (CONV_WIDTH - 1):])
        gla_p.append(s_p)

    y_prompt = _final_norm(h, norm_final[None], row0=0, n_rows=N_PROMPT, tile=512)
    y_sample = _final_norm(h, norm_final[None], row0=ROW_SAMPLE, n_rows=N_SAMPLE, tile=512)
    return (y_prompt.reshape(BATCH, SEQ, D_MODEL), y_sample.reshape(DEC_BATCH, DEC_SEQ, D_MODEL),
            jnp.stack(gla_p), jnp.stack(conv_p), gla_s, conv_s[:, :, SUBLANES - (CONV_WIDTH - 1):])
```

```python
import functools

import jax
import jax.numpy as jnp
from jax import lax
from jax.experimental import pallas as pl
from jax.experimental.pallas import tpu as pltpu

D_MODEL = 2048
BATCH = 4
SEQ = 2048
DEPTH = 4
DEC_BATCH = 128
DEC_SEQ = 8
N_META = 16
D_CONV = 1024
CONV_WIDTH = 3
CONV_GROUPS = 8
CONV_GROUP_WIDTH = D_CONV // CONV_GROUPS
GLA_HEADS = 4
GLA_DV = 256
GLA_DK = 128
GLA_GATE_RANK = 16
GLA_GATE_TAU = 16.0
GLA_CHUNK = 64
D_FF = 5632
EPS = 1e-6
D_QK = GLA_HEADS * GLA_DK
D_V = GLA_HEADS * GLA_DV
D_PROJ = 3 * D_CONV + 2 * D_QK + 2 * D_V

OFF_CB, OFF_CC, OFF_CH = 0, D_CONV, 2 * D_CONV
OFF_Q = 3 * D_CONV
OFF_K = OFF_Q + D_QK
OFF_V = OFF_K + D_QK
OFF_GO = OFF_V + D_V

N_PROMPT = BATCH * SEQ
N_SAMPLE = DEC_BATCH * DEC_SEQ
N_METAROWS = BATCH * N_META
ROW_SAMPLE = N_PROMPT
ROW_META = N_PROMPT + N_SAMPLE
N_TOK = ROW_META + N_METAROWS

SUBLANES = 8
LANES = 128
VMEM_LIMIT = 63 * 1024 * 1024

TM = 928
TF = 512
TN_MIX = 1024
TM_OUT = 464
SUB = 16
MAX_FACTORED_DECAY = 40.0
MIX_BLOCK = 256
SAMPLE_SEQS = 8

F32 = jnp.float32
BF16 = jnp.bfloat16


def _rms_scale(x, width):
    return lax.rsqrt(jnp.sum(x * x, axis=-1, keepdims=True) * (1.0 / width) + EPS)


def _sigmoid(x):
    return 1.0 / (1.0 + jnp.exp(-x))


def _ffn_kernel(h_ref, g_ref, wg_ref, wu_ref, wd_ref, o_ref, xn_ref):
    j = pl.program_id(1)

    @pl.when(j == 0)
    def _():
        h = h_ref[...]
        xn_ref[...] = (h * _rms_scale(h, D_MODEL) * g_ref[...]).astype(BF16)

    xn = xn_ref[...]
    a = jnp.dot(xn, wg_ref[...].astype(BF16), preferred_element_type=F32)
    b = jnp.dot(xn, wu_ref[...].astype(BF16), preferred_element_type=F32)
    act = (a * _sigmoid(a) * (b * 0.5)).astype(BF16)
    acc = jnp.where(j == 0, h_ref[...], o_ref[...])
    o_ref[...] = acc + jnp.dot(act, wd_ref[...].astype(BF16), preferred_element_type=F32)


def _ffn(h, gain, w_gu, w_down, layer):
    n_tok = h.shape[0]
    nf = D_FF // TF
    return pl.pallas_call(
        _ffn_kernel,
        grid=(n_tok // TM, nf),
        in_specs=[
            pl.BlockSpec((TM, D_MODEL), lambda i, j: (i, 0)),
            pl.BlockSpec((None, 1, D_MODEL), lambda i, j: (layer, 0, 0)),
            pl.BlockSpec((None, D_MODEL, TF), lambda i, j: (layer, 0, j)),
            pl.BlockSpec((None, D_MODEL, TF), lambda i, j: (layer, 0, j + nf)),
            pl.BlockSpec((None, TF, D_MODEL), lambda i, j: (layer, j, 0)),
        ],
        out_specs=pl.BlockSpec((TM, D_MODEL), lambda i, j: (i, 0)),
        out_shape=jax.ShapeDtypeStruct((n_tok, D_MODEL), F32),
        scratch_shapes=[pltpu.VMEM((TM, D_MODEL), BF16)],
        compiler_params=pltpu.CompilerParams(
            dimension_semantics=("arbitrary", "arbitrary"), vmem_limit_bytes=VMEM_LIMIT),
        name="ffn",
    )(h, gain, w_gu, w_gu, w_down)


_NT = (((1,), (1,)), ((), ()))


def _mixin_kernel(h_ref, g_ref, wt_ref, wflt_ref, fw2_ref, fb_ref, proj_ref, glog_ref, xn_ref):
    j = pl.program_id(1)
    half = pl.program_id(2)

    @pl.when(j == 0)
    def _():
        h = h_ref[...]
        xn = (h * _rms_scale(h, D_MODEL) * g_ref[...]).astype(BF16)
        xn_ref[half] = xn
        fl = lax.dot_general(xn, wflt_ref[...].astype(BF16), _NT, preferred_element_type=F32)
        fl = jnp.where(lax.broadcasted_iota(jnp.int32, fl.shape, 1) < GLA_GATE_RANK, fl, 0.0)
        z = jnp.dot(fl.astype(BF16), fw2_ref[...], preferred_element_type=F32) + fb_ref[...]
        glog_ref[...] = (jnp.minimum(z, 0.0) - jnp.log1p(jnp.exp(-jnp.abs(z)))) * (1.0 / GLA_GATE_TAU)

    proj_ref[...] = lax.dot_general(xn_ref[half], wt_ref[...].astype(BF16), _NT,
                                    preferred_element_type=F32).astype(proj_ref.dtype)


def _mixin(h, gain, w_in_t, fw2, fb, layer):
    n_tok = h.shape[0]
    assert n_tok % (2 * TM) == 0

    def tile(p, j, half):
        return jnp.where(j == 0, 2 * p + half, 2 * p + 1)

    return pl.pallas_call(
        _mixin_kernel,
        grid=(n_tok // (2 * TM), D_PROJ // TN_MIX, 2),
        in_specs=[
            pl.BlockSpec((TM, D_MODEL), lambda p, j, half: (tile(p, j, half), 0)),
            pl.BlockSpec((None, 1, D_MODEL), lambda p, j, half: (layer, 0, 0)),
            pl.BlockSpec((None, TN_MIX, D_MODEL), lambda p, j, half: (layer, j, 0)),
            pl.BlockSpec((None, LANES, D_MODEL), lambda p, j, half: (layer, D_PROJ // LANES, 0)),
            pl.BlockSpec((None, LANES, D_QK), lambda p, j, half: (layer, 0, 0)),
            pl.BlockSpec((None, 1, D_QK), lambda p, j, half: (layer, 0, 0)),
        ],
        out_specs=[
            pl.BlockSpec((TM, TN_MIX), lambda p, j, half: (2 * p + half, j)),
            pl.BlockSpec((TM, D_QK), lambda p, j, half: (tile(p, j, half), 0)),
        ],
        out_shape=[
            jax.ShapeDtypeStruct((n_tok, D_PROJ), BF16),
            jax.ShapeDtypeStruct((n_tok, D_QK), F32),
        ],
        scratch_shapes=[pltpu.VMEM((2, TM, D_MODEL), BF16)],
        compiler_params=pltpu.CompilerParams(
            dimension_semantics=("arbitrary", "arbitrary", "arbitrary"), vmem_limit_bytes=VMEM_LIMIT),
        name="mixin",
    )(h, gain, w_in_t, w_in_t, fw2, fb)


def _f32(proj_ref, rows, col0, width):
    return proj_ref[rows, col0:col0 + width].astype(F32)


def _block_rows(x, block, j):
    rows, width = x.shape
    if rows == block:
        return jnp.broadcast_to(x[j:j + 1, :], (rows, width))
    x3 = x.reshape(rows // block, block, width)
    return jnp.broadcast_to(x3[:, j:j + 1, :], (rows // block, block, width)).reshape(rows, width)


def _conv_branch(cb, u, u1, u2, cw_ref, cg_ref, y_ref, rows):
    conv = cw_ref[0:1, :] * u2 + cw_ref[1:2, :] * u1 + cw_ref[2:3, :] * u
    z = cb * conv
    for grp in range(CONV_GROUPS):
        cols = slice(grp * CONV_GROUP_WIDTH, (grp + 1) * CONV_GROUP_WIDTH)
        zg = z[:, cols]
        y_ref[rows, cols] = (zg * _rms_scale(zg, CONV_GROUP_WIDTH) * cg_ref[:, cols]).astype(y_ref.dtype)


def _segment_mask(rows, seg):
    ri = lax.broadcasted_iota(jnp.int32, (rows, rows), 0)
    ci = lax.broadcasted_iota(jnp.int32, (rows, rows), 1)
    causal = ci <= ri
    if rows != seg:
        causal = causal & ((ri // seg) == (ci // seg))
    return ri, ci, causal


def _cum_gates(g, seg):
    _, _, causal = _segment_mask(g.shape[0], seg)
    mcum = jnp.where(causal, 1.0, 0.0).astype(F32)
    return jnp.dot(mcum, g, preferred_element_type=F32, precision=lax.Precision.HIGHEST)


def _gla_rows(q, k, b, seg, sub, single_ref):
    rows = q.shape[0]
    ri, ci, causal = _segment_mask(rows, seg)
    qs = q * (GLA_DK ** -0.5)
    b_last = _block_rows(b, seg, seg - 1)
    qb = qs * jnp.exp(b)
    dec_last = jnp.exp(b_last)

    if single_ref:
        kinv = k * jnp.exp(-b)
        kdec = kinv * dec_last
        qb16, kinv16 = qb.astype(BF16), kinv.astype(BF16)
        a_heads = []
        for hd in range(GLA_HEADS):
            cols = slice(hd * GLA_DK, (hd + 1) * GLA_DK)
            a_h = lax.dot_general(qb16[:, cols], kinv16[:, cols], _NT, preferred_element_type=F32)
            a_heads.append(jnp.where(causal, a_h, 0.0))
        return qb, kdec, dec_last, a_heads

    blk0 = (ri // sub) * sub
    dcol = jnp.where(causal & (ci >= blk0), ci - blk0, -1)
    a_heads = [jnp.zeros((rows, rows), F32) for _ in range(GLA_HEADS)]
    for j in range(sub):
        kb = _block_rows(k, sub, j)
        bb = _block_rows(b, sub, j)
        p = qs * kb * jnp.exp(jnp.minimum(b - bb, 0.0))
        hit = dcol == j
        for hd in range(GLA_HEADS):
            r = jnp.sum(p[:, hd * GLA_DK:(hd + 1) * GLA_DK], axis=-1, keepdims=True)
            a_heads[hd] = jnp.where(hit, r, a_heads[hd])

    nsub = seg // sub
    if nsub > 1:
        assert rows == seg
        off = [[jnp.zeros((sub, rows), F32)] for _ in range(GLA_HEADS)]
        ci_s = lax.broadcasted_iota(jnp.int32, (sub, rows), 1)
        for i in range(1, nsub):
            lo = i * sub
            bs = b[lo - 1:lo, :]
            qd = (qs[lo:lo + sub, :] * jnp.exp(b[lo:lo + sub, :] - bs)).astype(BF16)
            kd = (k * jnp.exp(jnp.minimum(bs - b, 0.0))).astype(BF16)
            for hd in range(GLA_HEADS):
                cols = slice(hd * GLA_DK, (hd + 1) * GLA_DK)
                a_i = lax.dot_general(qd[:, cols], kd[:, cols], _NT, preferred_element_type=F32)
                off[hd].append(jnp.where(ci_s < lo, a_i, 0.0))
        a_heads = [a_heads[hd] + jnp.concatenate(off[hd], axis=0) for hd in range(GLA_HEADS)]

    kdec = k * jnp.exp(b_last - b)
    return qb, kdec, dec_last, a_heads


def _gla_out(o, go, gg_ref):
    return o * _rms_scale(o, GLA_DV) * gg_ref[...] * (go * _sigmoid(go))


def _mixer_chain_kernel(proj_ref, glog_ref, cw_ref, cg_ref, gg_ref, cprev_ref, s0_ref, y_in_ref,
                        y_ref, cnew_ref, snew_ref, s_scr, tail_scr, *, chunk, nchunks):
    del y_in_ref
    blk = pl.program_id(1)
    tb = chunk * nchunks
    sub = min(SUB, chunk)

    @pl.when(blk == 0)
    def _():
        s_scr[...] = s0_ref[0]
        tail_scr[...] = cprev_ref[0]

    u = _f32(proj_ref, slice(None), OFF_CC, D_CONV) * _f32(proj_ref, slice(None), OFF_CH, D_CONV)
    prev = tail_scr[...]
    t8 = lax.broadcasted_iota(jnp.int32, (SUBLANES, D_CONV), 0)
    r1 = pltpu.roll(u, 1, 0)
    r2 = pltpu.roll(u, 2, 0)
    f1 = jnp.where(t8 < 1, pltpu.roll(prev, 1, 0), r1[0:SUBLANES])
    f2 = jnp.where(t8 < 2, pltpu.roll(prev, 2, 0), r2[0:SUBLANES])
    if tb > SUBLANES:
        u1 = jnp.concatenate([f1, r1[SUBLANES:]], axis=0)
        u2 = jnp.concatenate([f2, r2[SUBLANES:]], axis=0)
    else:
        u1, u2 = f1, f2
    tail_scr[...] = u[tb - SUBLANES:tb]
    _conv_branch(_f32(proj_ref, slice(None), OFF_CB, D_CONV), u, u1, u2, cw_ref, cg_ref, y_ref, slice(None))

    b_blk = _cum_gates(glog_ref[...], chunk)

    def chunk_step(single_ref, c):
        rows = slice(c * chunk, (c + 1) * chunk)
        q = _f32(proj_ref, rows, OFF_Q, D_QK)
        k = _f32(proj_ref, rows, OFF_K, D_QK)
        qb, kdec, dec_last, a_heads = _gla_rows(q, k, b_blk[rows, :], chunk, sub, single_ref)
        for hd in range(GLA_HEADS):
            kc = slice(hd * GLA_DK, (hd + 1) * GLA_DK)
            v = proj_ref[rows, OFF_V + hd * GLA_DV:OFF_V + (hd + 1) * GLA_DV]
            go = _f32(proj_ref, rows, OFF_GO + hd * GLA_DV, GLA_DV)
            s = s_scr[hd]
            lhs = jnp.concatenate([qb[:, kc], a_heads[hd]], axis=1).astype(BF16)
            rhs = jnp.concatenate([s.astype(BF16), v], axis=0)
            o = jnp.dot(lhs, rhs, preferred_element_type=F32)
            y_ref[rows, D_CONV + hd * GLA_DV:D_CONV + (hd + 1) * GLA_DV] = (
                _gla_out(o, go, gg_ref).astype(y_ref.dtype))
            kdec_t = kdec[:, kc].T.astype(BF16)
            dec_col = jnp.broadcast_to(dec_last[0:1, kc], (SUBLANES, GLA_DK)).T[:, 0:1]
            s_scr[hd] = s * dec_col + jnp.dot(kdec_t, v, preferred_element_type=F32)

    def run_chunks(single_ref):
        for c in range(nchunks):
            chunk_step(single_ref, c)

    if sub == chunk:
        run_chunks(False)
    else:
        decay_bound = -chunk * jnp.min(glog_ref[...])
        lax.cond(decay_bound <= MAX_FACTORED_DECAY,
                 functools.partial(run_chunks, True), functools.partial(run_chunks, False))

    @pl.when(blk == pl.num_programs(1) - 1)
    def _():
        snew_ref[0] = s_scr[...]
        cnew_ref[0] = tail_scr[...]


def _mixer_chain(proj, glog, cw, cg, gg, cprev, s0, y_buf, layer, *, row0, n_seq, seq_len, chunk, block):
    nblk = seq_len // block
    base = row0 // block
    assert row0 % block == 0 and seq_len % block == 0 and block % chunk == 0
    kern = functools.partial(_mixer_chain_kernel, chunk=chunk, nchunks=block // chunk)
    return pl.pallas_call(
        kern,
        grid=(n_seq, nblk),
        in_specs=[
            pl.BlockSpec((block, D_PROJ), lambda s, i: (base + s * nblk + i, 0)),
            pl.BlockSpec((block, D_QK), lambda s, i: (base + s * nblk + i, 0)),
            pl.BlockSpec((None, CONV_WIDTH, D_CONV), lambda s, i: (layer, 0, 0)),
            pl.BlockSpec((None, 1, D_CONV), lambda s, i: (layer, 0, 0)),
            pl.BlockSpec((None, 1, GLA_DV), lambda s, i: (layer, 0, 0)),
            pl.BlockSpec((1, SUBLANES, D_CONV), lambda s, i: (s, 0, 0)),
            pl.BlockSpec((1, GLA_HEADS, GLA_DK, GLA_DV), lambda s, i: (s, 0, 0, 0)),
            pl.BlockSpec(memory_space=pl.ANY),
        ],
        out_specs=[
            pl.BlockSpec((block, D_MODEL), lambda s, i: (base + s * nblk + i, 0)),
            pl.BlockSpec((1, SUBLANES, D_CONV), lambda s, i: (s, 0, 0)),
            pl.BlockSpec((1, GLA_HEADS, GLA_DK, GLA_DV), lambda s, i: (s, 0, 0, 0)),
        ],
        out_shape=[
            jax.ShapeDtypeStruct(y_buf.shape, y_buf.dtype),
            jax.ShapeDtypeStruct((n_seq, SUBLANES, D_CONV), F32),
            jax.ShapeDtypeStruct((n_seq, GLA_HEADS, GLA_DK, GLA_DV), F32),
        ],
        scratch_shapes=[
            pltpu.VMEM((GLA_HEADS, GLA_DK, GLA_DV), F32),
            pltpu.VMEM((SUBLANES, D_CONV), F32),
        ],
        input_output_aliases={7: 0},
        compiler_params=pltpu.CompilerParams(
            dimension_semantics=("arbitrary", "arbitrary"), vmem_limit_bytes=VMEM_LIMIT),
        name=f"mixer_chain{chunk}",
    )(proj, glog, cw, cg, gg, cprev, s0, y_buf)


def _mixer_step_kernel(proj_ref, glog_ref, cw_ref, cg_ref, gg_ref, cprev_ref, s0_ref,
                       y_in_ref, cnew_in_ref, snew_in_ref, y_ref, cnew_ref, snew_ref, *, nseq, seq_len):
    del y_in_ref, cnew_in_ref, snew_in_ref
    rows = nseq * seq_len
    assert seq_len == SUBLANES

    u = _f32(proj_ref, slice(None), OFF_CC, D_CONV) * _f32(proj_ref, slice(None), OFF_CH, D_CONV)
    u3 = u.reshape(nseq, seq_len, D_CONV)
    prev = cprev_ref[...]
    t8 = lax.broadcasted_iota(jnp.int32, (nseq, seq_len, D_CONV), 1)
    u1 = jnp.where(t8 < 1, pltpu.roll(prev, 1, 1), pltpu.roll(u3, 1, 1)).reshape(rows, D_CONV)
    u2 = jnp.where(t8 < 2, pltpu.roll(prev, 2, 1), pltpu.roll(u3, 2, 1)).reshape(rows, D_CONV)
    cnew_ref[...] = u3
    _conv_branch(_f32(proj_ref, slice(None), OFF_CB, D_CONV), u, u1, u2, cw_ref, cg_ref, y_ref, slice(None))

    q = _f32(proj_ref, slice(None), OFF_Q, D_QK)
    k = _f32(proj_ref, slice(None), OFF_K, D_QK)
    b = _cum_gates(glog_ref[...], seq_len)
    qb, kdec, dec_last, a_heads = _gla_rows(q, k, b, seq_len, seq_len, False)

    qmask = (lax.broadcasted_iota(jnp.int32, (rows, nseq * GLA_DK), 0) // seq_len
             == lax.broadcasted_iota(jnp.int32, (rows, nseq * GLA_DK), 1) // GLA_DK)
    kmask = (lax.broadcasted_iota(jnp.int32, (nseq * GLA_DK, rows), 0) // GLA_DK
             == lax.broadcasted_iota(jnp.int32, (nseq * GLA_DK, rows), 1) // seq_len)
    for hd in range(GLA_HEADS):
        kc = slice(hd * GLA_DK, (hd + 1) * GLA_DK)
        v = proj_ref[:, OFF_V + hd * GLA_DV:OFF_V + (hd + 1) * GLA_DV]
        go = _f32(proj_ref, slice(None), OFF_GO + hd * GLA_DV, GLA_DV)
        s = s0_ref[:, hd].reshape(nseq * GLA_DK, GLA_DV)
        q_blk = jnp.where(qmask, jnp.tile(qb[:, kc], (1, nseq)), 0.0).astype(BF16)
        o = jnp.dot(q_blk, s.astype(BF16), preferred_element_type=F32)
        o = o + jnp.dot(a_heads[hd].astype(BF16), v, preferred_element_type=F32)
        y_ref[:, D_CONV + hd * GLA_DV:D_CONV + (hd + 1) * GLA_DV] = (
            _gla_out(o, go, gg_ref).astype(y_ref.dtype))
        kdec_t = kdec[:, kc].T
        k_blk = jnp.where(kmask, jnp.tile(kdec_t, (nseq, 1)), 0.0).astype(BF16)
        upd = jnp.dot(k_blk, v, preferred_element_type=F32)
        dec_t = dec_last[:, kc].T
        for n in range(nseq):
            dec_col = dec_t[:, n * seq_len:n * seq_len + 1]
            snew_ref[n, hd] = (s[n * GLA_DK:(n + 1) * GLA_DK, :] * dec_col
                               + upd[n * GLA_DK:(n + 1) * GLA_DK, :])


def _mixer_step(proj, glog, cw, cg, gg, state_conv, state_gla, y_buf, new_conv, new_gla, layer, *,
                row0, n_seq, seq_len):
    rows = SAMPLE_SEQS * seq_len
    base = row0 // rows
    assert row0 % rows == 0 and n_seq % SAMPLE_SEQS == 0
    kern = functools.partial(_mixer_step_kernel, nseq=SAMPLE_SEQS, seq_len=seq_len)
    return pl.pallas_call(
        kern,
        grid=(n_seq // SAMPLE_SEQS,),
        in_specs=[
            pl.BlockSpec((rows, D_PROJ), lambda i: (base + i, 0)),
            pl.BlockSpec((rows, D_QK), lambda i: (base + i, 0)),
            pl.BlockSpec((None, CONV_WIDTH, D_CONV), lambda i: (layer, 0, 0)),
            pl.BlockSpec((None, 1, D_CONV), lambda i: (layer, 0, 0)),
            pl.BlockSpec((None, 1, GLA_DV), lambda i: (layer, 0, 0)),
            pl.BlockSpec((None, SAMPLE_SEQS, SUBLANES, D_CONV), lambda i: (layer, i, 0, 0)),
            pl.BlockSpec((None, SAMPLE_SEQS, GLA_HEADS, GLA_DK, GLA_DV), lambda i: (layer, i, 0, 0, 0)),
        ] + [pl.BlockSpec(memory_space=pl.ANY)] * 3,
        out_specs=[
            pl.BlockSpec((rows, D_MODEL), lambda i: (base + i, 0)),
            pl.BlockSpec((None, SAMPLE_SEQS, SUBLANES, D_CONV), lambda i: (layer, i, 0, 0)),
            pl.BlockSpec((None, SAMPLE_SEQS, GLA_HEADS, GLA_DK, GLA_DV), lambda i: (layer, i, 0, 0, 0)),
        ],
        out_shape=[
            jax.ShapeDtypeStruct(y_buf.shape, y_buf.dtype),
            jax.ShapeDtypeStruct(state_conv.shape, F32),
            jax.ShapeDtypeStruct(state_gla.shape, F32),
        ],
        input_output_aliases={7: 0, 8: 1, 9: 2},
        compiler_params=pltpu.CompilerParams(
            dimension_semantics=("arbitrary",), vmem_limit_bytes=VMEM_LIMIT),
        name="mixer_step",
    )(proj, glog, cw, cg, gg, state_conv, state_gla, y_buf, new_conv, new_gla)


def _mixout_kernel(h_ref, y_ref, w_ref, o_ref):
    o_ref[...] = h_ref[...] + jnp.dot(y_ref[...], w_ref[...].astype(BF16), preferred_element_type=F32)


def _mixout(h, y, w_out, layer):
    n_tok = h.shape[0]
    return pl.pallas_call(
        _mixout_kernel,
        grid=(n_tok // TM_OUT,),
        in_specs=[
            pl.BlockSpec((TM_OUT, D_MODEL), lambda i: (i, 0)),
            pl.BlockSpec((TM_OUT, D_MODEL), lambda i: (i, 0)),
            pl.BlockSpec((None, D_MODEL, D_MODEL), lambda i: (layer, 0, 0), pipeline_mode=pl.Buffered(1)),
        ],
        out_specs=pl.BlockSpec((TM_OUT, D_MODEL), lambda i: (i, 0)),
        out_shape=jax.ShapeDtypeStruct((n_tok, D_MODEL), F32),
        compiler_params=pltpu.CompilerParams(
            dimension_semantics=("arbitrary",), vmem_limit_bytes=VMEM_LIMIT),
        name="mixout",
    )(h, y, w_out)


def _final_norm_kernel(h_ref, g_ref, o_ref):
    h = h_ref[...]
    o_ref[...] = h * _rms_scale(h, D_MODEL) * g_ref[...]


def _final_norm(h, gain, *, row0, n_rows, tile):
    base = row0 // tile
    assert row0 % tile == 0 and n_rows % tile == 0
    return pl.pallas_call(
        _final_norm_kernel,
        grid=(n_rows // tile,),
        in_specs=[
            pl.BlockSpec((tile, D_MODEL), lambda i: (base + i, 0)),
            pl.BlockSpec((1, D_MODEL), lambda i: (0, 0)),
        ],
        out_specs=pl.BlockSpec((tile, D_MODEL), lambda i: (i, 0)),
        out_shape=jax.ShapeDtypeStruct((n_rows, D_MODEL), F32),
        compiler_params=pltpu.CompilerParams(
            dimension_semantics=("arbitrary",), vmem_limit_bytes=VMEM_LIMIT),
        name="final_norm",
    )(h, gain)


def _pad_conv_state(c):
    return jnp.pad(c, ((0, 0),) * (c.ndim - 2) + ((SUBLANES - (CONV_WIDTH - 1), 0), (0, 0)))


def kernel(x_prompt, x_sample, state_conv, state_gla, meta_tokens, norm_ffn1, w_ffn1_gu, w_ffn1_down,
           norm_mix, w_mix_in, conv_w, conv_norm, gla_fgate_w2, gla_fgate_b, gla_out_norm, w_mix_out,
           norm_ffn2, w_ffn2_gu, w_ffn2_down, norm_final):
    assert x_prompt.shape == (BATCH, SEQ, D_MODEL) and x_sample.shape == (DEC_BATCH, DEC_SEQ, D_MODEL)
    h = jnp.concatenate([
        x_prompt.reshape(N_PROMPT, D_MODEL),
        x_sample.reshape(N_SAMPLE, D_MODEL),
        jnp.tile(meta_tokens.astype(x_prompt.dtype), (BATCH, 1)),
    ], axis=0)

    wgu1, wd1, wgu2, wd2, w_out = w_ffn1_gu, w_ffn1_down, w_ffn2_gu, w_ffn2_down, w_mix_out
    w_in = jnp.swapaxes(w_mix_in, 1, 2)
    fw2 = jnp.pad(gla_fgate_w2, ((0, 0), (0, LANES - GLA_GATE_RANK), (0, 0))).astype(BF16)
    n1, nm, n2 = (g.reshape(DEPTH, 1, D_MODEL) for g in (norm_ffn1, norm_mix, norm_ffn2))
    fb = gla_fgate_b.reshape(DEPTH, 1, D_QK)
    mix_w = (conv_w, conv_norm.reshape(DEPTH, 1, D_CONV), gla_out_norm.reshape(DEPTH, 1, GLA_DV))
    conv_s_in = _pad_conv_state(state_conv)

    zero_conv = jnp.zeros((BATCH, SUBLANES, D_CONV), F32)
    zero_gla = jnp.zeros((BATCH, GLA_HEADS, GLA_DK, GLA_DV), F32)
    y = jnp.zeros((N_TOK, D_MODEL), BF16)
    conv_s = jnp.zeros(conv_s_in.shape, F32)
    gla_s = jnp.zeros(state_gla.shape, F32)
    conv_p, gla_p = [], []
    for l in range(DEPTH):
        h = _ffn(h, n1, wgu1, wd1, l)
        proj, glog = _mixin(h, nm, w_in, fw2, fb, l)
        y, c_m, s_m = _mixer_chain(proj, glog, *mix_w, zero_conv, zero_gla, y, l, row0=ROW_META,
                                   n_seq=BATCH, seq_len=N_META, chunk=N_META, block=N_META)
        y, c_p, s_p = _mixer_chain(proj, glog, *mix_w, c_m, s_m, y, l, row0=0,
                                   n_seq=BATCH, seq_len=SEQ, chunk=GLA_CHUNK, block=MIX_BLOCK)
        y, conv_s, gla_s = _mixer_step(proj, glog, *mix_w, conv_s_in, state_gla, y, conv_s, gla_s, l,
                                       row0=ROW_SAMPLE, n_seq=DEC_BATCH, seq_len=DEC_SEQ)
        h = _mixout(h, y, w_out, l)
        h = _ffn(h, n2, wgu2, wd2, l)
        conv_p.append(c_p[:, SUBLANES - (CONV_WIDTH - 1):])
        gla_p.append(s_p)

    y_prompt = _final_norm(h, norm_final[None], row0=0, n_rows=N_PROMPT, tile=512)
    y_sample = _final_norm(h, norm_final[None], row0=ROW_SAMPLE, n_rows=N_SAMPLE, tile=512)
    return (y_prompt.reshape(BATCH, SEQ, D_MODEL), y_sample.reshape(DEC_BATCH, DEC_SEQ, D_MODEL),
            jnp.stack(gla_p), jnp.stack(conv_p), gla_s, conv_s[:, :, SUBLANES - (CONV_WIDTH - 1):])
```

```python
import functools

import jax
import jax.numpy as jnp
from jax import lax
from jax.experimental import pallas as pl
from jax.experimental.pallas import tpu as pltpu

D_MODEL = 2048
BATCH = 4
SEQ = 2048
DEPTH = 4
DEC_BATCH = 128
DEC_SEQ = 8
N_META = 16
D_CONV = 1024
CONV_WIDTH = 3
CONV_GROUPS = 8
CONV_GROUP_WIDTH = D_CONV // CONV_GROUPS
GLA_HEADS = 4
GLA_DV = 256
GLA_DK = 128
GLA_GATE_RANK = 16
GLA_GATE_TAU = 16.0
GLA_CHUNK = 64
D_FF = 5632
EPS = 1e-6
D_QK = GLA_HEADS * GLA_DK
D_V = GLA_HEADS * GLA_DV
D_PROJ = 3 * D_CONV + 2 * D_QK + 2 * D_V

OFF_CB, OFF_CC, OFF_CH = 0, D_CONV, 2 * D_CONV
OFF_Q = 3 * D_CONV
OFF_K = OFF_Q + D_QK
OFF_V = OFF_K + D_QK
OFF_GO = OFF_V + D_V

N_PROMPT = BATCH * SEQ
N_SAMPLE = DEC_BATCH * DEC_SEQ
N_METAROWS = BATCH * N_META
ROW_SAMPLE = N_PROMPT
ROW_META = N_PROMPT + N_SAMPLE
N_TOK = ROW_META + N_METAROWS

SUBLANES = 8
LANES = 128
VMEM_LIMIT = 63 * 1024 * 1024

TM = 928
TF = 512
TN_MIX = 1024
TM_OUT = 464
SUB = 16
MAX_FACTORED_DECAY = 40.0
MIX_BLOCK = 256
SAMPLE_SEQS = 8

F32 = jnp.float32
BF16 = jnp.bfloat16


def _rms_scale(x, width):
    return lax.rsqrt(jnp.sum(x * x, axis=-1, keepdims=True) * (1.0 / width) + EPS)


def _sigmoid(x):
    return 1.0 / (1.0 + jnp.exp(-x))


def _ffn_kernel(h_ref, g_ref, wg_ref, wu_ref, wd_ref, o_ref, xn_ref):
    j = pl.program_id(1)

    @pl.when(j == 0)
    def _():
        h = h_ref[...]
        xn_ref[...] = (h * _rms_scale(h, D_MODEL) * g_ref[...]).astype(BF16)

    xn = xn_ref[...]
    a = jnp.dot(xn, wg_ref[...].astype(BF16), preferred_element_type=F32)
    b = jnp.dot(xn, wu_ref[...].astype(BF16), preferred_element_type=F32)
    act = (a * _sigmoid(a) * (b * 0.5)).astype(BF16)
    acc = jnp.where(j == 0, h_ref[...], o_ref[...])
    o_ref[...] = acc + jnp.dot(act, wd_ref[...].astype(BF16), preferred_element_type=F32)


def _ffn(h, gain, w_gu, w_down, layer):
    n_tok = h.shape[0]
    nf = D_FF // TF
    return pl.pallas_call(
        _ffn_kernel,
        grid=(n_tok // TM, nf),
        in_specs=[
            pl.BlockSpec((TM, D_MODEL), lambda i, j: (i, 0)),
            pl.BlockSpec((None, 1, D_MODEL), lambda i, j: (layer, 0, 0)),
            pl.BlockSpec((None, D_MODEL, TF), lambda i, j: (layer, 0, j)),
            pl.BlockSpec((None, D_MODEL, TF), lambda i, j: (layer, 0, j + nf)),
            pl.BlockSpec((None, TF, D_MODEL), lambda i, j: (layer, j, 0)),
        ],
        out_specs=pl.BlockSpec((TM, D_MODEL), lambda i, j: (i, 0)),
        out_shape=jax.ShapeDtypeStruct((n_tok, D_MODEL), F32),
        scratch_shapes=[pltpu.VMEM((TM, D_MODEL), BF16)],
        compiler_params=pltpu.CompilerParams(
            dimension_semantics=("arbitrary", "arbitrary"), vmem_limit_bytes=VMEM_LIMIT),
        name="ffn",
    )(h, gain, w_gu, w_gu, w_down)


_NT = (((1,), (1,)), ((), ()))


def _mixin_kernel(h_ref, g_ref, wt_ref, wflt_ref, fw2_ref, fb_ref, proj_ref, glog_ref, xn_ref):
    j = pl.program_id(1)
    half = pl.program_id(2)

    @pl.when(j == 0)
    def _():
        h = h_ref[...]
        xn = (h * _rms_scale(h, D_MODEL) * g_ref[...]).astype(BF16)
        xn_ref[half] = xn
        fl = lax.dot_general(xn, wflt_ref[...].astype(BF16), _NT, preferred_element_type=F32)
        fl = jnp.where(lax.broadcasted_iota(jnp.int32, fl.shape, 1) < GLA_GATE_RANK, fl, 0.0)
        z = jnp.dot(fl.astype(BF16), fw2_ref[...], preferred_element_type=F32) + fb_ref[...]
        glog_ref[...] = (jnp.minimum(z, 0.0) - jnp.log1p(jnp.exp(-jnp.abs(z)))) * (1.0 / GLA_GATE_TAU)

    proj_ref[...] = lax.dot_general(xn_ref[half], wt_ref[...].astype(BF16), _NT,
                                    preferred_element_type=F32).astype(proj_ref.dtype)


def _mixin(h, gain, w_in_t, fw2, fb, layer):
    n_tok = h.shape[0]
    assert n_tok % (2 * TM) == 0

    def tile(p, j, half):
        return jnp.where(j == 0, 2 * p + half, 2 * p + 1)

    return pl.pallas_call(
        _mixin_kernel,
        grid=(n_tok // (2 * TM), D_PROJ // TN_MIX, 2),
        in_specs=[
            pl.BlockSpec((TM, D_MODEL), lambda p, j, half: (tile(p, j, half), 0)),
            pl.BlockSpec((None, 1, D_MODEL), lambda p, j, half: (layer, 0, 0)),
            pl.BlockSpec((None, TN_MIX, D_MODEL), lambda p, j, half: (layer, j, 0)),
            pl.BlockSpec((None, LANES, D_MODEL), lambda p, j, half: (layer, D_PROJ // LANES, 0)),
            pl.BlockSpec((None, LANES, D_QK), lambda p, j, half: (layer, 0, 0)),
            pl.BlockSpec((None, 1, D_QK), lambda p, j, half: (layer, 0, 0)),
        ],
        out_specs=[
            pl.BlockSpec((TM, TN_MIX), lambda p, j, half: (2 * p + half, j)),
            pl.BlockSpec((TM, D_QK), lambda p, j, half: (tile(p, j, half), 0)),
        ],
        out_shape=[
            jax.ShapeDtypeStruct((n_tok, D_PROJ), BF16),
            jax.ShapeDtypeStruct((n_tok, D_QK), F32),
        ],
        scratch_shapes=[pltpu.VMEM((2, TM, D_MODEL), BF16)],
        compiler_params=pltpu.CompilerParams(
            dimension_semantics=("arbitrary", "arbitrary", "arbitrary"), vmem_limit_bytes=VMEM_LIMIT),
        name="mixin",
    )(h, gain, w_in_t, w_in_t, fw2, fb)


def _f32(proj_ref, rows, col0, width):
    return proj_ref[rows, col0:col0 + width].astype(F32)


def _block_rows(x, block, j):
    rows, width = x.shape
    if rows == block:
        return jnp.broadcast_to(x[j:j + 1, :], (rows, width))
    x3 = x.reshape(rows // block, block, width)
    return jnp.broadcast_to(x3[:, j:j + 1, :], (rows // block, block, width)).reshape(rows, width)


def _conv_branch(cb, u, u1, u2, cw_ref, cg_ref, y_ref, rows):
    conv = cw_ref[0:1, :] * u2 + cw_ref[1:2, :] * u1 + cw_ref[2:3, :] * u
    z = cb * conv
    for grp in range(CONV_GROUPS):
        cols = slice(grp * CONV_GROUP_WIDTH, (grp + 1) * CONV_GROUP_WIDTH)
        zg = z[:, cols]
        y_ref[rows, cols] = (zg * _rms_scale(zg, CONV_GROUP_WIDTH) * cg_ref[:, cols]).astype(y_ref.dtype)


def _segment_mask(rows, seg):
    ri = lax.broadcasted_iota(jnp.int32, (rows, rows), 0)
    ci = lax.broadcasted_iota(jnp.int32, (rows, rows), 1)
    causal = ci <= ri
    if rows != seg:
        causal = causal & ((ri // seg) == (ci // seg))
    return ri, ci, causal


def _cum_gates(g, seg):
    _, _, causal = _segment_mask(g.shape[0], seg)
    mcum = jnp.where(causal, 1.0, 0.0).astype(BF16)
    hi = g.astype(BF16)
    rest = g - hi.astype(F32)
    mid = rest.astype(BF16)
    lo = (rest - mid.astype(F32)).astype(BF16)
    b = jnp.dot(mcum, lo, preferred_element_type=F32)
    b = b + jnp.dot(mcum, mid, preferred_element_type=F32)
    return b + jnp.dot(mcum, hi, preferred_element_type=F32)


def _gla_rows(q, k, b, seg, sub, single_ref):
    rows = q.shape[0]
    ri, ci, causal = _segment_mask(rows, seg)
    qs = q * (GLA_DK ** -0.5)
    b_last = _block_rows(b, seg, seg - 1)
    qb = qs * jnp.exp(b)
    dec_last = jnp.exp(b_last)

    if single_ref:
        kinv = k * jnp.exp(-b)
        kdec = kinv * dec_last
        qb16, kinv16 = qb.astype(BF16), kinv.astype(BF16)
        a_heads = []
        for hd in range(GLA_HEADS):
            cols = slice(hd * GLA_DK, (hd + 1) * GLA_DK)
            a_h = lax.dot_general(qb16[:, cols], kinv16[:, cols], _NT, preferred_element_type=F32)
            a_heads.append(jnp.where(causal, a_h, 0.0))
        return qb, kdec, dec_last, a_heads

    blk0 = (ri // sub) * sub
    dcol = jnp.where(causal & (ci >= blk0), ci - blk0, -1)
    a_heads = [jnp.zeros((rows, rows), F32) for _ in range(GLA_HEADS)]
    for j in range(sub):
        kb = _block_rows(k, sub, j)
        bb = _block_rows(b, sub, j)
        p = qs * kb * jnp.exp(jnp.minimum(b - bb, 0.0))
        hit = dcol == j
        for hd in range(GLA_HEADS):
            r = jnp.sum(p[:, hd * GLA_DK:(hd + 1) * GLA_DK], axis=-1, keepdims=True)
            a_heads[hd] = jnp.where(hit, r, a_heads[hd])

    nsub = seg // sub
    if nsub > 1:
        assert rows == seg
        off = [[jnp.zeros((sub, rows), F32)] for _ in range(GLA_HEADS)]
        ci_s = lax.broadcasted_iota(jnp.int32, (sub, rows), 1)
        for i in range(1, nsub):
            lo = i * sub
            bs = b[lo - 1:lo, :]
            qd = (qs[lo:lo + sub, :] * jnp.exp(b[lo:lo + sub, :] - bs)).astype(BF16)
            kd = (k * jnp.exp(jnp.minimum(bs - b, 0.0))).astype(BF16)
            for hd in range(GLA_HEADS):
                cols = slice(hd * GLA_DK, (hd + 1) * GLA_DK)
                a_i = lax.dot_general(qd[:, cols], kd[:, cols], _NT, preferred_element_type=F32)
                off[hd].append(jnp.where(ci_s < lo, a_i, 0.0))
        a_heads = [a_heads[hd] + jnp.concatenate(off[hd], axis=0) for hd in range(GLA_HEADS)]

    kdec = k * jnp.exp(b_last - b)
    return qb, kdec, dec_last, a_heads


def _gla_out(o, go, gg_ref):
    return o * _rms_scale(o, GLA_DV) * gg_ref[...] * (go * _sigmoid(go))


def _mixer_chain_kernel(proj_ref, glog_ref, cw_ref, cg_ref, gg_ref, cprev_ref, s0_ref, y_in_ref,
                        y_ref, cnew_ref, snew_ref, s_scr, tail_scr, *, chunk, nchunks):
    del y_in_ref
    blk = pl.program_id(1)
    tb = chunk * nchunks
    sub = min(SUB, chunk)

    @pl.when(blk == 0)
    def _():
        s_scr[...] = s0_ref[0]
        tail_scr[...] = cprev_ref[0]

    u = _f32(proj_ref, slice(None), OFF_CC, D_CONV) * _f32(proj_ref, slice(None), OFF_CH, D_CONV)
    prev = tail_scr[...]
    t8 = lax.broadcasted_iota(jnp.int32, (SUBLANES, D_CONV), 0)
    r1 = pltpu.roll(u, 1, 0)
    r2 = pltpu.roll(u, 2, 0)
    f1 = jnp.where(t8 < 1, pltpu.roll(prev, 1, 0), r1[0:SUBLANES])
    f2 = jnp.where(t8 < 2, pltpu.roll(prev, 2, 0), r2[0:SUBLANES])
    if tb > SUBLANES:
        u1 = jnp.concatenate([f1, r1[SUBLANES:]], axis=0)
        u2 = jnp.concatenate([f2, r2[SUBLANES:]], axis=0)
    else:
        u1, u2 = f1, f2
    tail_scr[...] = u[tb - SUBLANES:tb]
    _conv_branch(_f32(proj_ref, slice(None), OFF_CB, D_CONV), u, u1, u2, cw_ref, cg_ref, y_ref, slice(None))

    b_blk = _cum_gates(glog_ref[...], chunk)

    def chunk_step(single_ref, c):
        rows = slice(c * chunk, (c + 1) * chunk)
        q = _f32(proj_ref, rows, OFF_Q, D_QK)
        k = _f32(proj_ref, rows, OFF_K, D_QK)
        qb, kdec, dec_last, a_heads = _gla_rows(q, k, b_blk[rows, :], chunk, sub, single_ref)
        for hd in range(GLA_HEADS):
            kc = slice(hd * GLA_DK, (hd + 1) * GLA_DK)
            v = proj_ref[rows, OFF_V + hd * GLA_DV:OFF_V + (hd + 1) * GLA_DV]
            go = _f32(proj_ref, rows, OFF_GO + hd * GLA_DV, GLA_DV)
            s = s_scr[hd]
            lhs = jnp.concatenate([qb[:, kc], a_heads[hd]], axis=1).astype(BF16)
            rhs = jnp.concatenate([s.astype(BF16), v], axis=0)
            o = jnp.dot(lhs, rhs, preferred_element_type=F32)
            y_ref[rows, D_CONV + hd * GLA_DV:D_CONV + (hd + 1) * GLA_DV] = (
                _gla_out(o, go, gg_ref).astype(y_ref.dtype))
            kdec_t = kdec[:, kc].T.astype(BF16)
            dec_col = jnp.broadcast_to(dec_last[0:1, kc], (SUBLANES, GLA_DK)).T[:, 0:1]
            s_scr[hd] = s * dec_col + jnp.dot(kdec_t, v, preferred_element_type=F32)

    def run_chunks(single_ref):
        for c in range(nchunks):
            chunk_step(single_ref, c)

    if sub == chunk:
        run_chunks(False)
    else:
        decay_bound = -chunk * jnp.min(glog_ref[...])
        lax.cond(decay_bound <= MAX_FACTORED_DECAY,
                 functools.partial(run_chunks, True), functools.partial(run_chunks, False))

    @pl.when(blk == pl.num_programs(1) - 1)
    def _():
        snew_ref[0] = s_scr[...]
        cnew_ref[0] = tail_scr[...]


def _mixer_chain(proj, glog, cw, cg, gg, cprev, s0, y_buf, layer, *, row0, n_seq, seq_len, chunk, block):
    nblk = seq_len // block
    base = row0 // block
    assert row0 % block == 0 and seq_len % block == 0 and block % chunk == 0
    kern = functools.partial(_mixer_chain_kernel, chunk=chunk, nchunks=block // chunk)
    return pl.pallas_call(
        kern,
        grid=(n_seq, nblk),
        in_specs=[
            pl.BlockSpec((block, D_PROJ), lambda s, i: (base + s * nblk + i, 0)),
            pl.BlockSpec((block, D_QK), lambda s, i: (base + s * nblk + i, 0)),
            pl.BlockSpec((None, CONV_WIDTH, D_CONV), lambda s, i: (layer, 0, 0)),
            pl.BlockSpec((None, 1, D_CONV), lambda s, i: (layer, 0, 0)),
            pl.BlockSpec((None, 1, GLA_DV), lambda s, i: (layer, 0, 0)),
            pl.BlockSpec((1, SUBLANES, D_CONV), lambda s, i: (s, 0, 0)),
            pl.BlockSpec((1, GLA_HEADS, GLA_DK, GLA_DV), lambda s, i: (s, 0, 0, 0)),
            pl.BlockSpec(memory_space=pl.ANY),
        ],
        out_specs=[
            pl.BlockSpec((block, D_MODEL), lambda s, i: (base + s * nblk + i, 0)),
            pl.BlockSpec((1, SUBLANES, D_CONV), lambda s, i: (s, 0, 0)),
            pl.BlockSpec((1, GLA_HEADS, GLA_DK, GLA_DV), lambda s, i: (s, 0, 0, 0)),
        ],
        out_shape=[
            jax.ShapeDtypeStruct(y_buf.shape, y_buf.dtype),
            jax.ShapeDtypeStruct((n_seq, SUBLANES, D_CONV), F32),
            jax.ShapeDtypeStruct((n_seq, GLA_HEADS, GLA_DK, GLA_DV), F32),
        ],
        scratch_shapes=[
            pltpu.VMEM((GLA_HEADS, GLA_DK, GLA_DV), F32),
            pltpu.VMEM((SUBLANES, D_CONV), F32),
        ],
        input_output_aliases={7: 0},
        compiler_params=pltpu.CompilerParams(
            dimension_semantics=("arbitrary", "arbitrary"), vmem_limit_bytes=VMEM_LIMIT),
        name=f"mixer_chain{chunk}",
    )(proj, glog, cw, cg, gg, cprev, s0, y_buf)


def _mixer_step_kernel(proj_ref, glog_ref, cw_ref, cg_ref, gg_ref, cprev_ref, s0_ref,
                       y_in_ref, cnew_in_ref, snew_in_ref, y_ref, cnew_ref, snew_ref, *, nseq, seq_len):
    del y_in_ref, cnew_in_ref, snew_in_ref
    rows = nseq * seq_len
    assert seq_len == SUBLANES

    u = _f32(proj_ref, slice(None), OFF_CC, D_CONV) * _f32(proj_ref, slice(None), OFF_CH, D_CONV)
    u3 = u.reshape(nseq, seq_len, D_CONV)
    prev = jnp.concatenate([jnp.zeros((nseq, seq_len - (CONV_WIDTH - 1), D_CONV), F32), cprev_ref[...]], axis=1)
    t8 = lax.broadcasted_iota(jnp.int32, (nseq, seq_len, D_CONV), 1)
    u1 = jnp.where(t8 < 1, pltpu.roll(prev, 1, 1), pltpu.roll(u3, 1, 1)).reshape(rows, D_CONV)
    u2 = jnp.where(t8 < 2, pltpu.roll(prev, 2, 1), pltpu.roll(u3, 2, 1)).reshape(rows, D_CONV)
    cnew_ref[...] = u3[:, seq_len - (CONV_WIDTH - 1):, :]
    _conv_branch(_f32(proj_ref, slice(None), OFF_CB, D_CONV), u, u1, u2, cw_ref, cg_ref, y_ref, slice(None))

    q = _f32(proj_ref, slice(None), OFF_Q, D_QK)
    k = _f32(proj_ref, slice(None), OFF_K, D_QK)
    b = _cum_gates(glog_ref[...], seq_len)
    qb, kdec, dec_last, a_heads = _gla_rows(q, k, b, seq_len, seq_len, False)

    qmask = (lax.broadcasted_iota(jnp.int32, (rows, nseq * GLA_DK), 0) // seq_len
             == lax.broadcasted_iota(jnp.int32, (rows, nseq * GLA_DK), 1) // GLA_DK)
    kmask = (lax.broadcasted_iota(jnp.int32, (nseq * GLA_DK, rows), 0) // GLA_DK
             == lax.broadcasted_iota(jnp.int32, (nseq * GLA_DK, rows), 1) // seq_len)
    for hd in range(GLA_HEADS):
        kc = slice(hd * GLA_DK, (hd + 1) * GLA_DK)
        v = proj_ref[:, OFF_V + hd * GLA_DV:OFF_V + (hd + 1) * GLA_DV]
        go = _f32(proj_ref, slice(None), OFF_GO + hd * GLA_DV, GLA_DV)
        s = s0_ref[:, hd].reshape(nseq * GLA_DK, GLA_DV)
        q_blk = jnp.where(qmask, jnp.tile(qb[:, kc], (1, nseq)), 0.0).astype(BF16)
        o = jnp.dot(q_blk, s.astype(BF16), preferred_element_type=F32)
        o = o + jnp.dot(a_heads[hd].astype(BF16), v, preferred_element_type=F32)
        y_ref[:, D_CONV + hd * GLA_DV:D_CONV + (hd + 1) * GLA_DV] = (
            _gla_out(o, go, gg_ref).astype(y_ref.dtype))
        kdec_t = kdec[:, kc].T
        k_blk = jnp.where(kmask, jnp.tile(kdec_t, (nseq, 1)), 0.0).astype(BF16)
        upd = jnp.dot(k_blk, v, preferred_element_type=F32)
        dec_t = dec_last[:, kc].T
        for n in range(nseq):
            dec_col = dec_t[:, n * seq_len:n * seq_len + 1]
            snew_ref[n, hd] = (s[n * GLA_DK:(n + 1) * GLA_DK, :] * dec_col
                               + upd[n * GLA_DK:(n + 1) * GLA_DK, :])


def _mixer_step(proj, glog, cw, cg, gg, state_conv, state_gla, y_buf, new_conv, new_gla, layer, *,
                row0, n_seq, seq_len):
    rows = SAMPLE_SEQS * seq_len
    base = row0 // rows
    assert row0 % rows == 0 and n_seq % SAMPLE_SEQS == 0
    kern = functools.partial(_mixer_step_kernel, nseq=SAMPLE_SEQS, seq_len=seq_len)
    return pl.pallas_call(
        kern,
        grid=(n_seq // SAMPLE_SEQS,),
        in_specs=[
            pl.BlockSpec((rows, D_PROJ), lambda i: (base + i, 0)),
            pl.BlockSpec((rows, D_QK), lambda i: (base + i, 0)),
            pl.BlockSpec((None, CONV_WIDTH, D_CONV), lambda i: (layer, 0, 0)),
            pl.BlockSpec((None, 1, D_CONV), lambda i: (layer, 0, 0)),
            pl.BlockSpec((None, 1, GLA_DV), lambda i: (layer, 0, 0)),
            pl.BlockSpec((None, SAMPLE_SEQS, CONV_WIDTH - 1, D_CONV), lambda i: (layer, i, 0, 0)),
            pl.BlockSpec((None, SAMPLE_SEQS, GLA_HEADS, GLA_DK, GLA_DV), lambda i: (layer, i, 0, 0, 0)),
        ] + [pl.BlockSpec(memory_space=pl.ANY)] * 3,
        out_specs=[
            pl.BlockSpec((rows, D_MODEL), lambda i: (base + i, 0)),
            pl.BlockSpec((None, SAMPLE_SEQS, CONV_WIDTH - 1, D_CONV), lambda i: (layer, i, 0, 0)),
            pl.BlockSpec((None, SAMPLE_SEQS, GLA_HEADS, GLA_DK, GLA_DV), lambda i: (layer, i, 0, 0, 0)),
        ],
        out_shape=[
            jax.ShapeDtypeStruct(y_buf.shape, y_buf.dtype),
            jax.ShapeDtypeStruct(state_conv.shape, F32),
            jax.ShapeDtypeStruct(state_gla.shape, F32),
        ],
        input_output_aliases={7: 0, 8: 1, 9: 2},
        compiler_params=pltpu.CompilerParams(
            dimension_semantics=("arbitrary",), vmem_limit_bytes=VMEM_LIMIT),
        name="mixer_step",
    )(proj, glog, cw, cg, gg, state_conv, state_gla, y_buf, new_conv, new_gla)


def _mixout_kernel(h_ref, y_ref, w_ref, o_ref):
    o_ref[...] = h_ref[...] + jnp.dot(y_ref[...], w_ref[...].astype(BF16), preferred_element_type=F32)


def _mixout(h, y, w_out, layer):
    n_tok = h.shape[0]
    return pl.pallas_call(
        _mixout_kernel,
        grid=(n_tok // TM_OUT,),
        in_specs=[
            pl.BlockSpec((TM_OUT, D_MODEL), lambda i: (i, 0)),
            pl.BlockSpec((TM_OUT, D_MODEL), lambda i: (i, 0)),
            pl.BlockSpec((None, D_MODEL, D_MODEL), lambda i: (layer, 0, 0), pipeline_mode=pl.Buffered(1)),
        ],
        out_specs=pl.BlockSpec((TM_OUT, D_MODEL), lambda i: (i, 0)),
        out_shape=jax.ShapeDtypeStruct((n_tok, D_MODEL), F32),
        compiler_params=pltpu.CompilerParams(
            dimension_semantics=("arbitrary",), vmem_limit_bytes=VMEM_LIMIT),
        name="mixout",
    )(h, y, w_out)


def _final_norm_kernel(h_ref, g_ref, o_ref):
    h = h_ref[...]
    o_ref[...] = h * _rms_scale(h, D_MODEL) * g_ref[...]


def _final_norm(h, gain, *, row0, n_rows, tile):
    base = row0 // tile
    assert row0 % tile == 0 and n_rows % tile == 0
    return pl.pallas_call(
        _final_norm_kernel,
        grid=(n_rows // tile,),
        in_specs=[
            pl.BlockSpec((tile, D_MODEL), lambda i: (base + i, 0)),
            pl.BlockSpec((1, D_MODEL), lambda i: (0, 0)),
        ],
        out_specs=pl.BlockSpec((tile, D_MODEL), lambda i: (i, 0)),
        out_shape=jax.ShapeDtypeStruct((n_rows, D_MODEL), F32),
        compiler_params=pltpu.CompilerParams(
            dimension_semantics=("arbitrary",), vmem_limit_bytes=VMEM_LIMIT),
        name="final_norm",
    )(h, gain)


def kernel(x_prompt, x_sample, state_conv, state_gla, meta_tokens, norm_ffn1, w_ffn1_gu, w_ffn1_down,
           norm_mix, w_mix_in, conv_w, conv_norm, gla_fgate_w2, gla_fgate_b, gla_out_norm, w_mix_out,
           norm_ffn2, w_ffn2_gu, w_ffn2_down, norm_final):
    assert x_prompt.shape == (BATCH, SEQ, D_MODEL) and x_sample.shape == (DEC_BATCH, DEC_SEQ, D_MODEL)
    h = jnp.concatenate([
        x_prompt.reshape(N_PROMPT, D_MODEL),
        x_sample.reshape(N_SAMPLE, D_MODEL),
        jnp.tile(meta_tokens.astype(x_prompt.dtype), (BATCH, 1)),
    ], axis=0)

    wgu1, wd1, wgu2, wd2, w_out = w_ffn1_gu, w_ffn1_down, w_ffn2_gu, w_ffn2_down, w_mix_out
    w_in = jnp.swapaxes(w_mix_in, 1, 2)
    fw2 = jnp.pad(gla_fgate_w2, ((0, 0), (0, LANES - GLA_GATE_RANK), (0, 0))).astype(BF16)
    n1, nm, n2 = (g.reshape(DEPTH, 1, D_MODEL) for g in (norm_ffn1, norm_mix, norm_ffn2))
    fb = gla_fgate_b.reshape(DEPTH, 1, D_QK)
    mix_w = (conv_w, conv_norm.reshape(DEPTH, 1, D_CONV), gla_out_norm.reshape(DEPTH, 1, GLA_DV))

    zero_conv = jnp.zeros((BATCH, SUBLANES, D_CONV), F32)
    zero_gla = jnp.zeros((BATCH, GLA_HEADS, GLA_DK, GLA_DV), F32)
    y = jnp.zeros((N_TOK, D_MODEL), BF16)
    conv_s = jnp.zeros(state_conv.shape, F32)
    gla_s = jnp.zeros(state_gla.shape, F32)
    conv_p, gla_p = [], []
    for l in range(DEPTH):
        h = _ffn(h, n1, wgu1, wd1, l)
        proj, glog = _mixin(h, nm, w_in, fw2, fb, l)
        y, c_m, s_m = _mixer_chain(proj, glog, *mix_w, zero_conv, zero_gla, y, l, row0=ROW_META,
                                   n_seq=BATCH, seq_len=N_META, chunk=N_META, block=N_META)
        y, c_p, s_p = _mixer_chain(proj, glog, *mix_w, c_m, s_m, y, l, row0=0,
                                   n_seq=BATCH, seq_len=SEQ, chunk=GLA_CHUNK, block=MIX_BLOCK)
        y, conv_s, gla_s = _mixer_step(proj, glog, *mix_w, state_conv, state_gla, y, conv_s, gla_s, l,
                                       row0=ROW_SAMPLE, n_seq=DEC_BATCH, seq_len=DEC_SEQ)
        h = _mixout(h, y, w_out, l)
        h = _ffn(h, n2, wgu2, wd2, l)
        conv_p.append(c_p[:, SUBLANES - (CONV_WIDTH - 1):])
        gla_p.append(s_p)

    y_prompt = _final_norm(h, norm_final[None], row0=0, n_rows=N_PROMPT, tile=512)
    y_sample = _final_norm(h, norm_final[None], row0=ROW_SAMPLE, n_rows=N_SAMPLE, tile=512)
    return (y_prompt.reshape(BATCH, SEQ, D_MODEL), y_sample.reshape(DEC_BATCH, DEC_SEQ, D_MODEL),
            jnp.stack(gla_p), jnp.stack(conv_p), gla_s, conv_s)
```

```python
import functools

import jax
import jax.numpy as jnp
from jax import lax
from jax.experimental import pallas as pl
from jax.experimental.pallas import tpu as pltpu

D_MODEL = 2048
BATCH = 4
SEQ = 2048
DEPTH = 4
DEC_BATCH = 128
DEC_SEQ = 8
N_META = 16
D_CONV = 1024
CONV_WIDTH = 3
CONV_GROUPS = 8
CONV_GROUP_WIDTH = D_CONV // CONV_GROUPS
GLA_HEADS = 4
GLA_DV = 256
GLA_DK = 128
GLA_GATE_RANK = 16
GLA_GATE_TAU = 16.0
GLA_CHUNK = 64
D_FF = 5632
EPS = 1e-6
D_QK = GLA_HEADS * GLA_DK
D_V = GLA_HEADS * GLA_DV
D_PROJ = 3 * D_CONV + 2 * D_QK + 2 * D_V

OFF_CB, OFF_CC, OFF_CH = 0, D_CONV, 2 * D_CONV
OFF_Q = 3 * D_CONV
OFF_K = OFF_Q + D_QK
OFF_V = OFF_K + D_QK
OFF_GO = OFF_V + D_V

N_PROMPT = BATCH * SEQ
N_SAMPLE = DEC_BATCH * DEC_SEQ
N_METAROWS = BATCH * N_META
ROW_SAMPLE = N_PROMPT
ROW_META = N_PROMPT + N_SAMPLE
N_TOK = ROW_META + N_METAROWS

SUBLANES = 8
LANES = 128
VMEM_LIMIT = 63 * 1024 * 1024

TM = 928
TF = 512
TN_MIX = 1024
TM_OUT = 928
SUB = 16
MAX_FACTORED_DECAY = 40.0
MIX_BLOCK = 256
SAMPLE_SEQS = 8

F32 = jnp.float32
BF16 = jnp.bfloat16


def _rms_scale(x, width):
    return lax.rsqrt(jnp.sum(x * x, axis=-1, keepdims=True) * (1.0 / width) + EPS)


def _sigmoid(x):
    return 1.0 / (1.0 + jnp.exp(-x))


def _ffn_kernel(h_ref, g_ref, wg_ref, wu_ref, wd_ref, o_ref, xn_ref):
    j = pl.program_id(1)

    @pl.when(j == 0)
    def _():
        h = h_ref[...]
        xn_ref[...] = (h * _rms_scale(h, D_MODEL) * g_ref[...]).astype(BF16)

    xn = xn_ref[...]
    a = jnp.dot(xn, wg_ref[...].astype(BF16), preferred_element_type=F32)
    b = jnp.dot(xn, wu_ref[...].astype(BF16), preferred_element_type=F32)
    act = (a * _sigmoid(a) * (b * 0.5)).astype(BF16)
    acc = jnp.where(j == 0, h_ref[...], o_ref[...])
    o_ref[...] = acc + jnp.dot(act, wd_ref[...].astype(BF16), preferred_element_type=F32)


def _ffn(h, gain, w_gu, w_down, layer):
    n_tok = h.shape[0]
    nf = D_FF // TF
    return pl.pallas_call(
        _ffn_kernel,
        grid=(n_tok // TM, nf),
        in_specs=[
            pl.BlockSpec((TM, D_MODEL), lambda i, j: (i, 0)),
            pl.BlockSpec((None, 1, D_MODEL), lambda i, j: (layer, 0, 0)),
            pl.BlockSpec((None, D_MODEL, TF), lambda i, j: (layer, 0, j)),
            pl.BlockSpec((None, D_MODEL, TF), lambda i, j: (layer, 0, j + nf)),
            pl.BlockSpec((None, TF, D_MODEL), lambda i, j: (layer, j, 0)),
        ],
        out_specs=pl.BlockSpec((TM, D_MODEL), lambda i, j: (i, 0)),
        out_shape=jax.ShapeDtypeStruct((n_tok, D_MODEL), F32),
        scratch_shapes=[pltpu.VMEM((TM, D_MODEL), BF16)],
        compiler_params=pltpu.CompilerParams(
            dimension_semantics=("arbitrary", "arbitrary"), vmem_limit_bytes=VMEM_LIMIT),
        name="ffn",
    )(h, gain, w_gu, w_gu, w_down)


_NT = (((1,), (1,)), ((), ()))


def _mixin_kernel(h_ref, g_ref, wt_ref, wflt_ref, fw2_ref, fb_ref, proj_ref, glog_ref, xn_ref):
    j = pl.program_id(1)
    half = pl.program_id(2)

    @pl.when(j == 0)
    def _():
        h = h_ref[...]
        xn = (h * _rms_scale(h, D_MODEL) * g_ref[...]).astype(BF16)
        xn_ref[half] = xn
        fl = lax.dot_general(xn, wflt_ref[...].astype(BF16), _NT, preferred_element_type=F32)
        fl = jnp.where(lax.broadcasted_iota(jnp.int32, fl.shape, 1) < GLA_GATE_RANK, fl, 0.0)
        z = jnp.dot(fl.astype(BF16), fw2_ref[...], preferred_element_type=F32) + fb_ref[...]
        glog_ref[...] = (jnp.minimum(z, 0.0) - jnp.log1p(jnp.exp(-jnp.abs(z)))) * (1.0 / GLA_GATE_TAU)

    proj_ref[...] = lax.dot_general(xn_ref[half], wt_ref[...].astype(BF16), _NT,
                                    preferred_element_type=F32).astype(proj_ref.dtype)


def _mixin(h, gain, w_in_t, fw2, fb, layer):
    n_tok = h.shape[0]
    assert n_tok % (2 * TM) == 0

    def tile(p, j, half):
        return jnp.where(j == 0, 2 * p + half, 2 * p + 1)

    return pl.pallas_call(
        _mixin_kernel,
        grid=(n_tok // (2 * TM), D_PROJ // TN_MIX, 2),
        in_specs=[
            pl.BlockSpec((TM, D_MODEL), lambda p, j, half: (tile(p, j, half), 0)),
            pl.BlockSpec((None, 1, D_MODEL), lambda p, j, half: (layer, 0, 0)),
            pl.BlockSpec((None, TN_MIX, D_MODEL), lambda p, j, half: (layer, j, 0)),
            pl.BlockSpec((None, LANES, D_MODEL), lambda p, j, half: (layer, D_PROJ // LANES, 0)),
            pl.BlockSpec((None, LANES, D_QK), lambda p, j, half: (layer, 0, 0)),
            pl.BlockSpec((None, 1, D_QK), lambda p, j, half: (layer, 0, 0)),
        ],
        out_specs=[
            pl.BlockSpec((TM, TN_MIX), lambda p, j, half: (2 * p + half, j)),
            pl.BlockSpec((TM, D_QK), lambda p, j, half: (tile(p, j, half), 0)),
        ],
        out_shape=[
            jax.ShapeDtypeStruct((n_tok, D_PROJ), BF16),
            jax.ShapeDtypeStruct((n_tok, D_QK), F32),
        ],
        scratch_shapes=[pltpu.VMEM((2, TM, D_MODEL), BF16)],
        compiler_params=pltpu.CompilerParams(
            dimension_semantics=("arbitrary", "arbitrary", "arbitrary"), vmem_limit_bytes=VMEM_LIMIT),
        name="mixin",
    )(h, gain, w_in_t, w_in_t, fw2, fb)


def _f32(proj_ref, rows, col0, width):
    return proj_ref[rows, col0:col0 + width].astype(F32)


def _block_rows(x, block, j):
    rows, width = x.shape
    if rows == block:
        return jnp.broadcast_to(x[j:j + 1, :], (rows, width))
    x3 = x.reshape(rows // block, block, width)
    return jnp.broadcast_to(x3[:, j:j + 1, :], (rows // block, block, width)).reshape(rows, width)


def _conv_branch(cb, u, u1, u2, cw_ref, cg_ref, y_ref, rows):
    conv = cw_ref[0:1, :] * u2 + cw_ref[1:2, :] * u1 + cw_ref[2:3, :] * u
    z = cb * conv
    for grp in range(CONV_GROUPS):
        cols = slice(grp * CONV_GROUP_WIDTH, (grp + 1) * CONV_GROUP_WIDTH)
        zg = z[:, cols]
        y_ref[rows, cols] = (zg * _rms_scale(zg, CONV_GROUP_WIDTH) * cg_ref[:, cols]).astype(y_ref.dtype)


def _segment_mask(rows, seg):
    ri = lax.broadcasted_iota(jnp.int32, (rows, rows), 0)
    ci = lax.broadcasted_iota(jnp.int32, (rows, rows), 1)
    causal = ci <= ri
    if rows != seg:
        causal = causal & ((ri // seg) == (ci // seg))
    return ri, ci, causal


def _cum_gates(g, seg):
    _, _, causal = _segment_mask(g.shape[0], seg)
    mcum = jnp.where(causal, 1.0, 0.0).astype(BF16)
    hi = g.astype(BF16)
    rest = g - hi.astype(F32)
    mid = rest.astype(BF16)
    lo = (rest - mid.astype(F32)).astype(BF16)
    b = jnp.dot(mcum, lo, preferred_element_type=F32)
    b = b + jnp.dot(mcum, mid, preferred_element_type=F32)
    return b + jnp.dot(mcum, hi, preferred_element_type=F32)


def _gla_rows(q, k, b, seg, sub, single_ref):
    rows = q.shape[0]
    ri, ci, causal = _segment_mask(rows, seg)
    qs = q * (GLA_DK ** -0.5)
    b_last = _block_rows(b, seg, seg - 1)
    qb = qs * jnp.exp(b)
    dec_last = jnp.exp(b_last)

    if single_ref:
        kinv = k * jnp.exp(-b)
        kdec = kinv * dec_last
        qb16, kinv16 = qb.astype(BF16), kinv.astype(BF16)
        a_heads = []
        for hd in range(GLA_HEADS):
            cols = slice(hd * GLA_DK, (hd + 1) * GLA_DK)
            a_h = lax.dot_general(qb16[:, cols], kinv16[:, cols], _NT, preferred_element_type=F32)
            a_heads.append(jnp.where(causal, a_h, 0.0))
        return qb, kdec, dec_last, a_heads

    blk0 = (ri // sub) * sub
    dcol = jnp.where(causal & (ci >= blk0), ci - blk0, -1)
    a_heads = [jnp.zeros((rows, rows), F32) for _ in range(GLA_HEADS)]
    for j in range(sub):
        kb = _block_rows(k, sub, j)
        bb = _block_rows(b, sub, j)
        p = qs * kb * jnp.exp(jnp.minimum(b - bb, 0.0))
        hit = dcol == j
        for hd in range(GLA_HEADS):
            r = jnp.sum(p[:, hd * GLA_DK:(hd + 1) * GLA_DK], axis=-1, keepdims=True)
            a_heads[hd] = jnp.where(hit, r, a_heads[hd])

    nsub = seg // sub
    if nsub > 1:
        assert rows == seg
        off = [[jnp.zeros((sub, rows), F32)] for _ in range(GLA_HEADS)]
        ci_s = lax.broadcasted_iota(jnp.int32, (sub, rows), 1)
        for i in range(1, nsub):
            lo = i * sub
            bs = b[lo - 1:lo, :]
            qd = (qs[lo:lo + sub, :] * jnp.exp(b[lo:lo + sub, :] - bs)).astype(BF16)
            kd = (k * jnp.exp(jnp.minimum(bs - b, 0.0))).astype(BF16)
            for hd in range(GLA_HEADS):
                cols = slice(hd * GLA_DK, (hd + 1) * GLA_DK)
                a_i = lax.dot_general(qd[:, cols], kd[:, cols], _NT, preferred_element_type=F32)
                off[hd].append(jnp.where(ci_s < lo, a_i, 0.0))
        a_heads = [a_heads[hd] + jnp.concatenate(off[hd], axis=0) for hd in range(GLA_HEADS)]

    kdec = k * jnp.exp(b_last - b)
    return qb, kdec, dec_last, a_heads


def _gla_out(o, go, gg_ref):
    return o * _rms_scale(o, GLA_DV) * gg_ref[...] * (go * _sigmoid(go))


def _mixer_chain_kernel(proj_ref, glog_ref, cw_ref, cg_ref, gg_ref, cprev_ref, s0_ref, y_in_ref,
                        y_ref, cnew_ref, snew_ref, s_scr, tail_scr, *, chunk, nchunks):
    del y_in_ref
    blk = pl.program_id(1)
    tb = chunk * nchunks
    sub = min(SUB, chunk)

    @pl.when(blk == 0)
    def _():
        s_scr[...] = s0_ref[0]
        tail_scr[...] = cprev_ref[0]

    u = _f32(proj_ref, slice(None), OFF_CC, D_CONV) * _f32(proj_ref, slice(None), OFF_CH, D_CONV)
    prev = tail_scr[...]
    t8 = lax.broadcasted_iota(jnp.int32, (SUBLANES, D_CONV), 0)
    r1 = pltpu.roll(u, 1, 0)
    r2 = pltpu.roll(u, 2, 0)
    f1 = jnp.where(t8 < 1, pltpu.roll(prev, 1, 0), r1[0:SUBLANES])
    f2 = jnp.where(t8 < 2, pltpu.roll(prev, 2, 0), r2[0:SUBLANES])
    if tb > SUBLANES:
        u1 = jnp.concatenate([f1, r1[SUBLANES:]], axis=0)
        u2 = jnp.concatenate([f2, r2[SUBLANES:]], axis=0)
    else:
        u1, u2 = f1, f2
    tail_scr[...] = u[tb - SUBLANES:tb]
    _conv_branch(_f32(proj_ref, slice(None), OFF_CB, D_CONV), u, u1, u2, cw_ref, cg_ref, y_ref, slice(None))

    b_blk = _cum_gates(glog_ref[...], chunk)

    def chunk_step(single_ref, c):
        rows = slice(c * chunk, (c + 1) * chunk)
        q = _f32(proj_ref, rows, OFF_Q, D_QK)
        k = _f32(proj_ref, rows, OFF_K, D_QK)
        qb, kdec, dec_last, a_heads = _gla_rows(q, k, b_blk[rows, :], chunk, sub, single_ref)
        for hd in range(GLA_HEADS):
            kc = slice(hd * GLA_DK, (hd + 1) * GLA_DK)
            v = proj_ref[rows, OFF_V + hd * GLA_DV:OFF_V + (hd + 1) * GLA_DV]
            go = _f32(proj_ref, rows, OFF_GO + hd * GLA_DV, GLA_DV)
            s = s_scr[hd]
            lhs = jnp.concatenate([qb[:, kc], a_heads[hd]], axis=1).astype(BF16)
            rhs = jnp.concatenate([s.astype(BF16), v], axis=0)
            o = jnp.dot(lhs, rhs, preferred_element_type=F32)
            y_ref[rows, D_CONV + hd * GLA_DV:D_CONV + (hd + 1) * GLA_DV] = (
                _gla_out(o, go, gg_ref).astype(y_ref.dtype))
            kdec_t = kdec[:, kc].T.astype(BF16)
            dec_col = jnp.broadcast_to(dec_last[0:1, kc], (SUBLANES, GLA_DK)).T[:, 0:1]
            s_scr[hd] = s * dec_col + jnp.dot(kdec_t, v, preferred_element_type=F32)

    def run_chunks(single_ref):
        for c in range(nchunks):
            chunk_step(single_ref, c)

    if sub == chunk:
        run_chunks(False)
    else:
        decay_bound = -chunk * jnp.min(glog_ref[...])
        lax.cond(decay_bound <= MAX_FACTORED_DECAY,
                 functools.partial(run_chunks, True), functools.partial(run_chunks, False))

    @pl.when(blk == pl.num_programs(1) - 1)
    def _():
        snew_ref[0] = s_scr[...]
        cnew_ref[0] = tail_scr[...]


def _mixer_chain(proj, glog, cw, cg, gg, cprev, s0, y_buf, layer, *, row0, n_seq, seq_len, chunk, block):
    nblk = seq_len // block
    base = row0 // block
    assert row0 % block == 0 and seq_len % block == 0 and block % chunk == 0
    kern = functools.partial(_mixer_chain_kernel, chunk=chunk, nchunks=block // chunk)
    return pl.pallas_call(
        kern,
        grid=(n_seq, nblk),
        in_specs=[
            pl.BlockSpec((block, D_PROJ), lambda s, i: (base + s * nblk + i, 0)),
            pl.BlockSpec((block, D_QK), lambda s, i: (base + s * nblk + i, 0)),
            pl.BlockSpec((None, CONV_WIDTH, D_CONV), lambda s, i: (layer, 0, 0)),
            pl.BlockSpec((None, 1, D_CONV), lambda s, i: (layer, 0, 0)),
            pl.BlockSpec((None, 1, GLA_DV), lambda s, i: (layer, 0, 0)),
            pl.BlockSpec((1, SUBLANES, D_CONV), lambda s, i: (s, 0, 0)),
            pl.BlockSpec((1, GLA_HEADS, GLA_DK, GLA_DV), lambda s, i: (s, 0, 0, 0)),
            pl.BlockSpec(memory_space=pl.ANY),
        ],
        out_specs=[
            pl.BlockSpec((block, D_MODEL), lambda s, i: (base + s * nblk + i, 0)),
            pl.BlockSpec((1, SUBLANES, D_CONV), lambda s, i: (s, 0, 0)),
            pl.BlockSpec((1, GLA_HEADS, GLA_DK, GLA_DV), lambda s, i: (s, 0, 0, 0)),
        ],
        out_shape=[
            jax.ShapeDtypeStruct(y_buf.shape, y_buf.dtype),
            jax.ShapeDtypeStruct((n_seq, SUBLANES, D_CONV), F32),
            jax.ShapeDtypeStruct((n_seq, GLA_HEADS, GLA_DK, GLA_DV), F32),
        ],
        scratch_shapes=[
            pltpu.VMEM((GLA_HEADS, GLA_DK, GLA_DV), F32),
            pltpu.VMEM((SUBLANES, D_CONV), F32),
        ],
        input_output_aliases={7: 0},
        compiler_params=pltpu.CompilerParams(
            dimension_semantics=("arbitrary", "arbitrary"), vmem_limit_bytes=VMEM_LIMIT),
        name=f"mixer_chain{chunk}",
    )(proj, glog, cw, cg, gg, cprev, s0, y_buf)


def _mixer_step_kernel(proj_ref, glog_ref, cw_ref, cg_ref, gg_ref, cprev_ref, s0_ref,
                       y_in_ref, cnew_in_ref, snew_in_ref, y_ref, cnew_ref, snew_ref, *, nseq, seq_len):
    del y_in_ref, cnew_in_ref, snew_in_ref
    rows = nseq * seq_len
    assert seq_len == SUBLANES

    u = _f32(proj_ref, slice(None), OFF_CC, D_CONV) * _f32(proj_ref, slice(None), OFF_CH, D_CONV)
    u3 = u.reshape(nseq, seq_len, D_CONV)
    prev = jnp.concatenate([jnp.zeros((nseq, seq_len - (CONV_WIDTH - 1), D_CONV), F32), cprev_ref[...]], axis=1)
    t8 = lax.broadcasted_iota(jnp.int32, (nseq, seq_len, D_CONV), 1)
    u1 = jnp.where(t8 < 1, pltpu.roll(prev, 1, 1), pltpu.roll(u3, 1, 1)).reshape(rows, D_CONV)
    u2 = jnp.where(t8 < 2, pltpu.roll(prev, 2, 1), pltpu.roll(u3, 2, 1)).reshape(rows, D_CONV)
    cnew_ref[...] = u3[:, seq_len - (CONV_WIDTH - 1):, :]
    _conv_branch(_f32(proj_ref, slice(None), OFF_CB, D_CONV), u, u1, u2, cw_ref, cg_ref, y_ref, slice(None))

    q = _f32(proj_ref, slice(None), OFF_Q, D_QK)
    k = _f32(proj_ref, slice(None), OFF_K, D_QK)
    b = _cum_gates(glog_ref[...], seq_len)
    qb, kdec, dec_last, a_heads = _gla_rows(q, k, b, seq_len, seq_len, False)

    qmask = (lax.broadcasted_iota(jnp.int32, (rows, nseq * GLA_DK), 0) // seq_len
             == lax.broadcasted_iota(jnp.int32, (rows, nseq * GLA_DK), 1) // GLA_DK)
    kmask = (lax.broadcasted_iota(jnp.int32, (nseq * GLA_DK, rows), 0) // GLA_DK
             == lax.broadcasted_iota(jnp.int32, (nseq * GLA_DK, rows), 1) // seq_len)
    for hd in range(GLA_HEADS):
        kc = slice(hd * GLA_DK, (hd + 1) * GLA_DK)
        v = proj_ref[:, OFF_V + hd * GLA_DV:OFF_V + (hd + 1) * GLA_DV]
        go = _f32(proj_ref, slice(None), OFF_GO + hd * GLA_DV, GLA_DV)
        s = s0_ref[:, hd].reshape(nseq * GLA_DK, GLA_DV)
        q_blk = jnp.where(qmask, jnp.tile(qb[:, kc], (1, nseq)), 0.0).astype(BF16)
        o = jnp.dot(q_blk, s.astype(BF16), preferred_element_type=F32)
        o = o + jnp.dot(a_heads[hd].astype(BF16), v, preferred_element_type=F32)
        y_ref[:, D_CONV + hd * GLA_DV:D_CONV + (hd + 1) * GLA_DV] = (
            _gla_out(o, go, gg_ref).astype(y_ref.dtype))
        kdec_t = kdec[:, kc].T
        k_blk = jnp.where(kmask, jnp.tile(kdec_t, (nseq, 1)), 0.0).astype(BF16)
        upd = jnp.dot(k_blk, v, preferred_element_type=F32)
        dec_t = dec_last[:, kc].T
        for n in range(nseq):
            dec_col = dec_t[:, n * seq_len:n * seq_len + 1]
            snew_ref[n, hd] = (s[n * GLA_DK:(n + 1) * GLA_DK, :] * dec_col
                               + upd[n * GLA_DK:(n + 1) * GLA_DK, :])


def _mixer_step(proj, glog, cw, cg, gg, state_conv, state_gla, y_buf, new_conv, new_gla, layer, *,
                row0, n_seq, seq_len):
    rows = SAMPLE_SEQS * seq_len
    base = row0 // rows
    assert row0 % rows == 0 and n_seq % SAMPLE_SEQS == 0
    kern = functools.partial(_mixer_step_kernel, nseq=SAMPLE_SEQS, seq_len=seq_len)
    return pl.pallas_call(
        kern,
        grid=(n_seq // SAMPLE_SEQS,),
        in_specs=[
            pl.BlockSpec((rows, D_PROJ), lambda i: (base + i, 0)),
            pl.BlockSpec((rows, D_QK), lambda i: (base + i, 0)),
            pl.BlockSpec((None, CONV_WIDTH, D_CONV), lambda i: (layer, 0, 0)),
            pl.BlockSpec((None, 1, D_CONV), lambda i: (layer, 0, 0)),
            pl.BlockSpec((None, 1, GLA_DV), lambda i: (layer, 0, 0)),
            pl.BlockSpec((None, SAMPLE_SEQS, CONV_WIDTH - 1, D_CONV), lambda i: (layer, i, 0, 0)),
            pl.BlockSpec((None, SAMPLE_SEQS, GLA_HEADS, GLA_DK, GLA_DV), lambda i: (layer, i, 0, 0, 0)),
        ] + [pl.BlockSpec(memory_space=pl.ANY)] * 3,
        out_specs=[
            pl.BlockSpec((rows, D_MODEL), lambda i: (base + i, 0)),
            pl.BlockSpec((None, SAMPLE_SEQS, CONV_WIDTH - 1, D_CONV), lambda i: (layer, i, 0, 0)),
            pl.BlockSpec((None, SAMPLE_SEQS, GLA_HEADS, GLA_DK, GLA_DV), lambda i: (layer, i, 0, 0, 0)),
        ],
        out_shape=[
            jax.ShapeDtypeStruct(y_buf.shape, y_buf.dtype),
            jax.ShapeDtypeStruct(state_conv.shape, F32),
            jax.ShapeDtypeStruct(state_gla.shape, F32),
        ],
        input_output_aliases={7: 0, 8: 1, 9: 2},
        compiler_params=pltpu.CompilerParams(
            dimension_semantics=("arbitrary",), vmem_limit_bytes=VMEM_LIMIT),
        name="mixer_step",
    )(proj, glog, cw, cg, gg, state_conv, state_gla, y_buf, new_conv, new_gla)


def _mixout_kernel(h_ref, y_ref, w_ref, o_ref):
    o_ref[...] = h_ref[...] + jnp.dot(y_ref[...], w_ref[...].astype(BF16), preferred_element_type=F32)


def _mixout(h, y, w_out, layer):
    n_tok = h.shape[0]
    return pl.pallas_call(
        _mixout_kernel,
        grid=(n_tok // TM_OUT,),
        in_specs=[
            pl.BlockSpec((TM_OUT, D_MODEL), lambda i: (i, 0)),
            pl.BlockSpec((TM_OUT, D_MODEL), lambda i: (i, 0)),
            pl.BlockSpec((None, D_MODEL, D_MODEL), lambda i: (layer, 0, 0), pipeline_mode=pl.Buffered(1)),
        ],
        out_specs=pl.BlockSpec((TM_OUT, D_MODEL), lambda i: (i, 0)),
        out_shape=jax.ShapeDtypeStruct((n_tok, D_MODEL), F32),
        compiler_params=pltpu.CompilerParams(
            dimension_semantics=("arbitrary",), vmem_limit_bytes=VMEM_LIMIT),
        name="mixout",
    )(h, y, w_out)


def _final_norm_kernel(h_ref, g_ref, o_ref):
    h = h_ref[...]
    o_ref[...] = h * _rms_scale(h, D_MODEL) * g_ref[...]


def _final_norm(h, gain, *, row0, n_rows, tile):
    base = row0 // tile
    assert row0 % tile == 0 and n_rows % tile == 0
    return pl.pallas_call(
        _final_norm_kernel,
        grid=(n_rows // tile,),
        in_specs=[
            pl.BlockSpec((tile, D_MODEL), lambda i: (base + i, 0)),
            pl.BlockSpec((1, D_MODEL), lambda i: (0, 0)),
        ],
        out_specs=pl.BlockSpec((tile, D_MODEL), lambda i: (i, 0)),
        out_shape=jax.ShapeDtypeStruct((n_rows, D_MODEL), F32),
        compiler_params=pltpu.CompilerParams(
            dimension_semantics=("arbitrary",), vmem_limit_bytes=VMEM_LIMIT),
        name="final_norm",
    )(h, gain)


def kernel(x_prompt, x_sample, state_conv, state_gla, meta_tokens, norm_ffn1, w_ffn1_gu, w_ffn1_down,
           norm_mix, w_mix_in, conv_w, conv_norm, gla_fgate_w2, gla_fgate_b, gla_out_norm, w_mix_out,
           norm_ffn2, w_ffn2_gu, w_ffn2_down, norm_final):
    assert x_prompt.shape == (BATCH, SEQ, D_MODEL) and x_sample.shape == (DEC_BATCH, DEC_SEQ, D_MODEL)
    h = jnp.concatenate([
        x_prompt.reshape(N_PROMPT, D_MODEL),
        x_sample.reshape(N_SAMPLE, D_MODEL),
        jnp.tile(meta_tokens.astype(x_prompt.dtype), (BATCH, 1)),
    ], axis=0)

    wgu1, wd1, wgu2, wd2, w_out = w_ffn1_gu, w_ffn1_down, w_ffn2_gu, w_ffn2_down, w_mix_out
    w_in = jnp.swapaxes(w_mix_in, 1, 2)
    fw2 = jnp.pad(gla_fgate_w2, ((0, 0), (0, LANES - GLA_GATE_RANK), (0, 0))).astype(BF16)
    n1, nm, n2 = (g.reshape(DEPTH, 1, D_MODEL) for g in (norm_ffn1, norm_mix, norm_ffn2))
    fb = gla_fgate_b.reshape(DEPTH, 1, D_QK)
    mix_w = (conv_w, conv_norm.reshape(DEPTH, 1, D_CONV), gla_out_norm.reshape(DEPTH, 1, GLA_DV))

    zero_conv = jnp.zeros((BATCH, SUBLANES, D_CONV), F32)
    zero_gla = jnp.zeros((BATCH, GLA_HEADS, GLA_DK, GLA_DV), F32)
    y = jnp.zeros((N_TOK, D_MODEL), BF16)
    conv_s = jnp.zeros(state_conv.shape, F32)
    gla_s = jnp.zeros(state_gla.shape, F32)
    conv_p, gla_p = [], []
    for l in range(DEPTH):
        h = _ffn(h, n1, wgu1, wd1, l)
        proj, glog = _mixin(h, nm, w_in, fw2, fb, l)
        y, c_m, s_m = _mixer_chain(proj, glog, *mix_w, zero_conv, zero_gla, y, l, row0=ROW_META,
                                   n_seq=BATCH, seq_len=N_META, chunk=N_META, block=N_META)
        y, c_p, s_p = _mixer_chain(proj, glog, *mix_w, c_m, s_m, y, l, row0=0,
                                   n_seq=BATCH, seq_len=SEQ, chunk=GLA_CHUNK, block=MIX_BLOCK)
        y, conv_s, gla_s = _mixer_step(proj, glog, *mix_w, state_conv, state_gla, y, conv_s, gla_s, l,
                                       row0=ROW_SAMPLE, n_seq=DEC_BATCH, seq_len=DEC_SEQ)
        h = _mixout(h, y, w_out, l)
        h = _ffn(h, n2, wgu2, wd2, l)
        conv_p.append(c_p[:, SUBLANES - (CONV_WIDTH - 1):])
        gla_p.append(s_p)

    y_prompt = _final_norm(h, norm_final[None], row0=0, n_rows=N_PROMPT, tile=1024)
    y_sample = _final_norm(h, norm_final[None], row0=ROW_SAMPLE, n_rows=N_SAMPLE, tile=1024)
    return (y_prompt.reshape(BATCH, SEQ, D_MODEL), y_sample.reshape(DEC_BATCH, DEC_SEQ, D_MODEL),
            jnp.stack(gla_p), jnp.stack(conv_p), gla_s, conv_s)
```

```python
import functools

import jax
import jax.numpy as jnp
from jax import lax
from jax.experimental import pallas as pl
from jax.experimental.pallas import tpu as pltpu

D_MODEL = 2048
BATCH = 4
SEQ = 2048
DEPTH = 4
DEC_BATCH = 128
DEC_SEQ = 8
N_META = 16
D_CONV = 1024
CONV_WIDTH = 3
CONV_GROUPS = 8
CONV_GROUP_WIDTH = D_CONV // CONV_GROUPS
GLA_HEADS = 4
GLA_DV = 256
GLA_DK = 128
GLA_GATE_RANK = 16
GLA_GATE_TAU = 16.0
GLA_CHUNK = 64
D_FF = 5632
EPS = 1e-6
D_QK = GLA_HEADS * GLA_DK
D_V = GLA_HEADS * GLA_DV
D_PROJ = 3 * D_CONV + 2 * D_QK + 2 * D_V

OFF_CB, OFF_CC, OFF_CH = 0, D_CONV, 2 * D_CONV
OFF_Q = 3 * D_CONV
OFF_K = OFF_Q + D_QK
OFF_V = OFF_K + D_QK
OFF_GO = OFF_V + D_V

N_PROMPT = BATCH * SEQ
N_SAMPLE = DEC_BATCH * DEC_SEQ
N_METAROWS = BATCH * N_META
ROW_SAMPLE = N_PROMPT
ROW_META = N_PROMPT + N_SAMPLE
N_TOK = ROW_META + N_METAROWS

SUBLANES = 8
LANES = 128
VMEM_LIMIT = 63 * 1024 * 1024

TM = 928
TF = 512
TN_MIX = 1024
TM_OUT = 464
SUB = 16
MAX_FACTORED_DECAY = 40.0
MIX_BLOCK = 256
SAMPLE_SEQS = 16

F32 = jnp.float32
BF16 = jnp.bfloat16


def _rms_scale(x, width):
    return lax.rsqrt(jnp.sum(x * x, axis=-1, keepdims=True) * (1.0 / width) + EPS)


def _sigmoid(x):
    return 1.0 / (1.0 + jnp.exp(-x))


def _ffn_kernel(h_ref, g_ref, wg_ref, wu_ref, wd_ref, o_ref, xn_ref):
    j = pl.program_id(1)

    @pl.when(j == 0)
    def _():
        h = h_ref[...]
        xn_ref[...] = (h * _rms_scale(h, D_MODEL) * g_ref[...]).astype(BF16)

    xn = xn_ref[...]
    a = jnp.dot(xn, wg_ref[...].astype(BF16), preferred_element_type=F32)
    b = jnp.dot(xn, wu_ref[...].astype(BF16), preferred_element_type=F32)
    act = (a * _sigmoid(a) * (b * 0.5)).astype(BF16)
    acc = jnp.where(j == 0, h_ref[...], o_ref[...])
    o_ref[...] = acc + jnp.dot(act, wd_ref[...].astype(BF16), preferred_element_type=F32)


def _ffn(h, gain, w_gu, w_down, layer):
    n_tok = h.shape[0]
    nf = D_FF // TF
    return pl.pallas_call(
        _ffn_kernel,
        grid=(n_tok // TM, nf),
        in_specs=[
            pl.BlockSpec((TM, D_MODEL), lambda i, j: (i, 0)),
            pl.BlockSpec((None, 1, D_MODEL), lambda i, j: (layer, 0, 0)),
            pl.BlockSpec((None, D_MODEL, TF), lambda i, j: (layer, 0, j)),
            pl.BlockSpec((None, D_MODEL, TF), lambda i, j: (layer, 0, j + nf)),
            pl.BlockSpec((None, TF, D_MODEL), lambda i, j: (layer, j, 0)),
        ],
        out_specs=pl.BlockSpec((TM, D_MODEL), lambda i, j: (i, 0)),
        out_shape=jax.ShapeDtypeStruct((n_tok, D_MODEL), F32),
        scratch_shapes=[pltpu.VMEM((TM, D_MODEL), BF16)],
        compiler_params=pltpu.CompilerParams(
            dimension_semantics=("arbitrary", "arbitrary"), vmem_limit_bytes=VMEM_LIMIT),
        name="ffn",
    )(h, gain, w_gu, w_gu, w_down)


_NT = (((1,), (1,)), ((), ()))


def _mixin_kernel(h_ref, g_ref, wt_ref, wflt_ref, fw2_ref, fb_ref, proj_ref, glog_ref, xn_ref):
    j = pl.program_id(1)
    half = pl.program_id(2)

    @pl.when(j == 0)
    def _():
        h = h_ref[...]
        xn = (h * _rms_scale(h, D_MODEL) * g_ref[...]).astype(BF16)
        xn_ref[half] = xn
        fl = lax.dot_general(xn, wflt_ref[...].astype(BF16), _NT, preferred_element_type=F32)
        fl = jnp.where(lax.broadcasted_iota(jnp.int32, fl.shape, 1) < GLA_GATE_RANK, fl, 0.0)
        z = jnp.dot(fl.astype(BF16), fw2_ref[...], preferred_element_type=F32) + fb_ref[...]
        glog_ref[...] = (jnp.minimum(z, 0.0) - jnp.log1p(jnp.exp(-jnp.abs(z)))) * (1.0 / GLA_GATE_TAU)

    proj_ref[...] = lax.dot_general(xn_ref[half], wt_ref[...].astype(BF16), _NT,
                                    preferred_element_type=F32).astype(proj_ref.dtype)


def _mixin(h, gain, w_in_t, fw2, fb, layer):
    n_tok = h.shape[0]
    assert n_tok % (2 * TM) == 0

    def tile(p, j, half):
        return jnp.where(j == 0, 2 * p + half, 2 * p + 1)

    return pl.pallas_call(
        _mixin_kernel,
        grid=(n_tok // (2 * TM), D_PROJ // TN_MIX, 2),
        in_specs=[
            pl.BlockSpec((TM, D_MODEL), lambda p, j, half: (tile(p, j, half), 0)),
            pl.BlockSpec((None, 1, D_MODEL), lambda p, j, half: (layer, 0, 0)),
            pl.BlockSpec((None, TN_MIX, D_MODEL), lambda p, j, half: (layer, j, 0)),
            pl.BlockSpec((None, LANES, D_MODEL), lambda p, j, half: (layer, D_PROJ // LANES, 0)),
            pl.BlockSpec((None, LANES, D_QK), lambda p, j, half: (layer, 0, 0)),
            pl.BlockSpec((None, 1, D_QK), lambda p, j, half: (layer, 0, 0)),
        ],
        out_specs=[
            pl.BlockSpec((TM, TN_MIX), lambda p, j, half: (2 * p + half, j)),
            pl.BlockSpec((TM, D_QK), lambda p, j, half: (tile(p, j, half), 0)),
        ],
        out_shape=[
            jax.ShapeDtypeStruct((n_tok, D_PROJ), BF16),
            jax.ShapeDtypeStruct((n_tok, D_QK), F32),
        ],
        scratch_shapes=[pltpu.VMEM((2, TM, D_MODEL), BF16)],
        compiler_params=pltpu.CompilerParams(
            dimension_semantics=("arbitrary", "arbitrary", "arbitrary"), vmem_limit_bytes=VMEM_LIMIT),
        name="mixin",
    )(h, gain, w_in_t, w_in_t, fw2, fb)


def _f32(proj_ref, rows, col0, width):
    return proj_ref[rows, col0:col0 + width].astype(F32)


def _block_rows(x, block, j):
    rows, width = x.shape
    if rows == block:
        return jnp.broadcast_to(x[j:j + 1, :], (rows, width))
    x3 = x.reshape(rows // block, block, width)
    return jnp.broadcast_to(x3[:, j:j + 1, :], (rows // block, block, width)).reshape(rows, width)


def _conv_branch(cb, u, u1, u2, cw_ref, cg_ref, y_ref, rows):
    conv = cw_ref[0:1, :] * u2 + cw_ref[1:2, :] * u1 + cw_ref[2:3, :] * u
    z = cb * conv
    for grp in range(CONV_GROUPS):
        cols = slice(grp * CONV_GROUP_WIDTH, (grp + 1) * CONV_GROUP_WIDTH)
        zg = z[:, cols]
        y_ref[rows, cols] = (zg * _rms_scale(zg, CONV_GROUP_WIDTH) * cg_ref[:, cols]).astype(y_ref.dtype)


def _segment_mask(rows, seg):
    ri = lax.broadcasted_iota(jnp.int32, (rows, rows), 0)
    ci = lax.broadcasted_iota(jnp.int32, (rows, rows), 1)
    causal = ci <= ri
    if rows != seg:
        causal = causal & ((ri // seg) == (ci // seg))
    return ri, ci, causal


def _cum_gates(g, seg):
    _, _, causal = _segment_mask(g.shape[0], seg)
    mcum = jnp.where(causal, 1.0, 0.0).astype(BF16)
    hi = g.astype(BF16)
    rest = g - hi.astype(F32)
    mid = rest.astype(BF16)
    lo = (rest - mid.astype(F32)).astype(BF16)
    b = jnp.dot(mcum, lo, preferred_element_type=F32)
    b = b + jnp.dot(mcum, mid, preferred_element_type=F32)
    return b + jnp.dot(mcum, hi, preferred_element_type=F32)


def _gla_rows(q, k, b, seg, sub, single_ref):
    rows = q.shape[0]
    ri, ci, causal = _segment_mask(rows, seg)
    qs = q * (GLA_DK ** -0.5)
    b_last = _block_rows(b, seg, seg - 1)
    qb = qs * jnp.exp(b)
    dec_last = jnp.exp(b_last)

    if single_ref:
        kinv = k * jnp.exp(-b)
        kdec = kinv * dec_last
        qb16, kinv16 = qb.astype(BF16), kinv.astype(BF16)
        a_heads = []
        for hd in range(GLA_HEADS):
            cols = slice(hd * GLA_DK, (hd + 1) * GLA_DK)
            a_h = lax.dot_general(qb16[:, cols], kinv16[:, cols], _NT, preferred_element_type=F32)
            a_heads.append(jnp.where(causal, a_h, 0.0))
        return qb, kdec, dec_last, a_heads

    blk0 = (ri // sub) * sub
    dcol = jnp.where(causal & (ci >= blk0), ci - blk0, -1)
    a_heads = [jnp.zeros((rows, rows), F32) for _ in range(GLA_HEADS)]
    for j in range(sub):
        kb = _block_rows(k, sub, j)
        bb = _block_rows(b, sub, j)
        p = qs * kb * jnp.exp(jnp.minimum(b - bb, 0.0))
        hit = dcol == j
        for hd in range(GLA_HEADS):
            r = jnp.sum(p[:, hd * GLA_DK:(hd + 1) * GLA_DK], axis=-1, keepdims=True)
            a_heads[hd] = jnp.where(hit, r, a_heads[hd])

    nsub = seg // sub
    if nsub > 1:
        assert rows == seg
        off = [[jnp.zeros((sub, rows), F32)] for _ in range(GLA_HEADS)]
        ci_s = lax.broadcasted_iota(jnp.int32, (sub, rows), 1)
        for i in range(1, nsub):
            lo = i * sub
            bs = b[lo - 1:lo, :]
            qd = (qs[lo:lo + sub, :] * jnp.exp(b[lo:lo + sub, :] - bs)).astype(BF16)
            kd = (k * jnp.exp(jnp.minimum(bs - b, 0.0))).astype(BF16)
            for hd in range(GLA_HEADS):
                cols = slice(hd * GLA_DK, (hd + 1) * GLA_DK)
                a_i = lax.dot_general(qd[:, cols], kd[:, cols], _NT, preferred_element_type=F32)
                off[hd].append(jnp.where(ci_s < lo, a_i, 0.0))
        a_heads = [a_heads[hd] + jnp.concatenate(off[hd], axis=0) for hd in range(GLA_HEADS)]

    kdec = k * jnp.exp(b_last - b)
    return qb, kdec, dec_last, a_heads


def _gla_out(o, go, gg_ref):
    return o * _rms_scale(o, GLA_DV) * gg_ref[...] * (go * _sigmoid(go))


def _mixer_chain_kernel(proj_ref, glog_ref, cw_ref, cg_ref, gg_ref, cprev_ref, s0_ref, y_in_ref,
                        y_ref, cnew_ref, snew_ref, s_scr, tail_scr, *, chunk, nchunks):
    del y_in_ref
    blk = pl.program_id(1)
    tb = chunk * nchunks
    sub = min(SUB, chunk)

    @pl.when(blk == 0)
    def _():
        s_scr[...] = s0_ref[0]
        tail_scr[...] = cprev_ref[0]

    u = _f32(proj_ref, slice(None), OFF_CC, D_CONV) * _f32(proj_ref, slice(None), OFF_CH, D_CONV)
    prev = tail_scr[...]
    t8 = lax.broadcasted_iota(jnp.int32, (SUBLANES, D_CONV), 0)
    r1 = pltpu.roll(u, 1, 0)
    r2 = pltpu.roll(u, 2, 0)
    f1 = jnp.where(t8 < 1, pltpu.roll(prev, 1, 0), r1[0:SUBLANES])
    f2 = jnp.where(t8 < 2, pltpu.roll(prev, 2, 0), r2[0:SUBLANES])
    if tb > SUBLANES:
        u1 = jnp.concatenate([f1, r1[SUBLANES:]], axis=0)
        u2 = jnp.concatenate([f2, r2[SUBLANES:]], axis=0)
    else:
        u1, u2 = f1, f2
    tail_scr[...] = u[tb - SUBLANES:tb]
    _conv_branch(_f32(proj_ref, slice(None), OFF_CB, D_CONV), u, u1, u2, cw_ref, cg_ref, y_ref, slice(None))

    b_blk = _cum_gates(glog_ref[...], chunk)

    def chunk_step(single_ref, c):
        rows = slice(c * chunk, (c + 1) * chunk)
        q = _f32(proj_ref, rows, OFF_Q, D_QK)
        k = _f32(proj_ref, rows, OFF_K, D_QK)
        qb, kdec, dec_last, a_heads = _gla_rows(q, k, b_blk[rows, :], chunk, sub, single_ref)
        for hd in range(GLA_HEADS):
            kc = slice(hd * GLA_DK, (hd + 1) * GLA_DK)
            v = proj_ref[rows, OFF_V + hd * GLA_DV:OFF_V + (hd + 1) * GLA_DV]
            go = _f32(proj_ref, rows, OFF_GO + hd * GLA_DV, GLA_DV)
            s = s_scr[hd]
            lhs = jnp.concatenate([qb[:, kc], a_heads[hd]], axis=1).astype(BF16)
            rhs = jnp.concatenate([s.astype(BF16), v], axis=0)
            o = jnp.dot(lhs, rhs, preferred_element_type=F32)
            y_ref[rows, D_CONV + hd * GLA_DV:D_CONV + (hd + 1) * GLA_DV] = (
                _gla_out(o, go, gg_ref).astype(y_ref.dtype))
            kdec_t = kdec[:, kc].T.astype(BF16)
            dec_col = jnp.broadcast_to(dec_last[0:1, kc], (SUBLANES, GLA_DK)).T[:, 0:1]
            s_scr[hd] = s * dec_col + jnp.dot(kdec_t, v, preferred_element_type=F32)

    def run_chunks(single_ref):
        for c in range(nchunks):
            chunk_step(single_ref, c)

    if sub == chunk:
        run_chunks(False)
    else:
        decay_bound = -chunk * jnp.min(glog_ref[...])
        lax.cond(decay_bound <= MAX_FACTORED_DECAY,
                 functools.partial(run_chunks, True), functools.partial(run_chunks, False))

    @pl.when(blk == pl.num_programs(1) - 1)
    def _():
        snew_ref[0] = s_scr[...]
        cnew_ref[0] = tail_scr[...]


def _mixer_chain(proj, glog, cw, cg, gg, cprev, s0, y_buf, layer, *, row0, n_seq, seq_len, chunk, block):
    nblk = seq_len // block
    base = row0 // block
    assert row0 % block == 0 and seq_len % block == 0 and block % chunk == 0
    kern = functools.partial(_mixer_chain_kernel, chunk=chunk, nchunks=block // chunk)
    return pl.pallas_call(
        kern,
        grid=(n_seq, nblk),
        in_specs=[
            pl.BlockSpec((block, D_PROJ), lambda s, i: (base + s * nblk + i, 0)),
            pl.BlockSpec((block, D_QK), lambda s, i: (base + s * nblk + i, 0)),
            pl.BlockSpec((None, CONV_WIDTH, D_CONV), lambda s, i: (layer, 0, 0)),
            pl.BlockSpec((None, 1, D_CONV), lambda s, i: (layer, 0, 0)),
            pl.BlockSpec((None, 1, GLA_DV), lambda s, i: (layer, 0, 0)),
            pl.BlockSpec((1, SUBLANES, D_CONV), lambda s, i: (s, 0, 0)),
            pl.BlockSpec((1, GLA_HEADS, GLA_DK, GLA_DV), lambda s, i: (s, 0, 0, 0)),
            pl.BlockSpec(memory_space=pl.ANY),
        ],
        out_specs=[
            pl.BlockSpec((block, D_MODEL), lambda s, i: (base + s * nblk + i, 0)),
            pl.BlockSpec((1, SUBLANES, D_CONV), lambda s, i: (s, 0, 0)),
            pl.BlockSpec((1, GLA_HEADS, GLA_DK, GLA_DV), lambda s, i: (s, 0, 0, 0)),
        ],
        out_shape=[
            jax.ShapeDtypeStruct(y_buf.shape, y_buf.dtype),
            jax.ShapeDtypeStruct((n_seq, SUBLANES, D_CONV), F32),
            jax.ShapeDtypeStruct((n_seq, GLA_HEADS, GLA_DK, GLA_DV), F32),
        ],
        scratch_shapes=[
            pltpu.VMEM((GLA_HEADS, GLA_DK, GLA_DV), F32),
            pltpu.VMEM((SUBLANES, D_CONV), F32),
        ],
        input_output_aliases={7: 0},
        compiler_params=pltpu.CompilerParams(
            dimension_semantics=("arbitrary", "arbitrary"), vmem_limit_bytes=VMEM_LIMIT),
        name=f"mixer_chain{chunk}",
    )(proj, glog, cw, cg, gg, cprev, s0, y_buf)


def _mixer_step_kernel(proj_ref, glog_ref, cw_ref, cg_ref, gg_ref, cprev_ref, s0_ref,
                       y_in_ref, cnew_in_ref, snew_in_ref, y_ref, cnew_ref, snew_ref, *, nseq, seq_len):
    del y_in_ref, cnew_in_ref, snew_in_ref
    rows = nseq * seq_len
    assert seq_len == SUBLANES

    u = _f32(proj_ref, slice(None), OFF_CC, D_CONV) * _f32(proj_ref, slice(None), OFF_CH, D_CONV)
    u3 = u.reshape(nseq, seq_len, D_CONV)
    prev = jnp.concatenate([jnp.zeros((nseq, seq_len - (CONV_WIDTH - 1), D_CONV), F32), cprev_ref[...]], axis=1)
    t8 = lax.broadcasted_iota(jnp.int32, (nseq, seq_len, D_CONV), 1)
    u1 = jnp.where(t8 < 1, pltpu.roll(prev, 1, 1), pltpu.roll(u3, 1, 1)).reshape(rows, D_CONV)
    u2 = jnp.where(t8 < 2, pltpu.roll(prev, 2, 1), pltpu.roll(u3, 2, 1)).reshape(rows, D_CONV)
    cnew_ref[...] = u3[:, seq_len - (CONV_WIDTH - 1):, :]
    _conv_branch(_f32(proj_ref, slice(None), OFF_CB, D_CONV), u, u1, u2, cw_ref, cg_ref, y_ref, slice(None))

    q = _f32(proj_ref, slice(None), OFF_Q, D_QK)
    k = _f32(proj_ref, slice(None), OFF_K, D_QK)
    b = _cum_gates(glog_ref[...], seq_len)
    qb, kdec, dec_last, a_heads = _gla_rows(q, k, b, seq_len, seq_len, False)

    qmask = (lax.broadcasted_iota(jnp.int32, (rows, nseq * GLA_DK), 0) // seq_len
             == lax.broadcasted_iota(jnp.int32, (rows, nseq * GLA_DK), 1) // GLA_DK)
    kmask = (lax.broadcasted_iota(jnp.int32, (nseq * GLA_DK, rows), 0) // GLA_DK
             == lax.broadcasted_iota(jnp.int32, (nseq * GLA_DK, rows), 1) // seq_len)
    for hd in range(GLA_HEADS):
        kc = slice(hd * GLA_DK, (hd + 1) * GLA_DK)
        v = proj_ref[:, OFF_V + hd * GLA_DV:OFF_V + (hd + 1) * GLA_DV]
        go = _f32(proj_ref, slice(None), OFF_GO + hd * GLA_DV, GLA_DV)
        s = s0_ref[:, hd].reshape(nseq * GLA_DK, GLA_DV)
        q_blk = jnp.where(qmask, jnp.tile(qb[:, kc], (1, nseq)), 0.0).astype(BF16)
        o = jnp.dot(q_blk, s.astype(BF16), preferred_element_type=F32)
        o = o + jnp.dot(a_heads[hd].astype(BF16), v, preferred_element_type=F32)
        y_ref[:, D_CONV + hd * GLA_DV:D_CONV + (hd + 1) * GLA_DV] = (
            _gla_out(o, go, gg_ref).astype(y_ref.dtype))
        kdec_t = kdec[:, kc].T
        k_blk = jnp.where(kmask, jnp.tile(kdec_t, (nseq, 1)), 0.0).astype(BF16)
        upd = jnp.dot(k_blk, v, preferred_element_type=F32)
        dec_t = dec_last[:, kc].T
        for n in range(nseq):
            dec_col = dec_t[:, n * seq_len:n * seq_len + 1]
            snew_ref[n, hd] = (s[n * GLA_DK:(n + 1) * GLA_DK, :] * dec_col
                               + upd[n * GLA_DK:(n + 1) * GLA_DK, :])


def _mixer_step(proj, glog, cw, cg, gg, state_conv, state_gla, y_buf, new_conv, new_gla, layer, *,
                row0, n_seq, seq_len):
    rows = SAMPLE_SEQS * seq_len
    base = row0 // rows
    assert row0 % rows == 0 and n_seq % SAMPLE_SEQS == 0
    kern = functools.partial(_mixer_step_kernel, nseq=SAMPLE_SEQS, seq_len=seq_len)
    return pl.pallas_call(
        kern,
        grid=(n_seq // SAMPLE_SEQS,),
        in_specs=[
            pl.BlockSpec((rows, D_PROJ), lambda i: (base + i, 0)),
            pl.BlockSpec((rows, D_QK), lambda i: (base + i, 0)),
            pl.BlockSpec((None, CONV_WIDTH, D_CONV), lambda i: (layer, 0, 0)),
            pl.BlockSpec((None, 1, D_CONV), lambda i: (layer, 0, 0)),
            pl.BlockSpec((None, 1, GLA_DV), lambda i: (layer, 0, 0)),
            pl.BlockSpec((None, SAMPLE_SEQS, CONV_WIDTH - 1, D_CONV), lambda i: (layer, i, 0, 0)),
            pl.BlockSpec((None, SAMPLE_SEQS, GLA_HEADS, GLA_DK, GLA_DV), lambda i: (layer, i, 0, 0, 0)),
        ] + [pl.BlockSpec(memory_space=pl.ANY)] * 3,
        out_specs=[
            pl.BlockSpec((rows, D_MODEL), lambda i: (base + i, 0)),
            pl.BlockSpec((None, SAMPLE_SEQS, CONV_WIDTH - 1, D_CONV), lambda i: (layer, i, 0, 0)),
            pl.BlockSpec((None, SAMPLE_SEQS, GLA_HEADS, GLA_DK, GLA_DV), lambda i: (layer, i, 0, 0, 0)),
        ],
        out_shape=[
            jax.ShapeDtypeStruct(y_buf.shape, y_buf.dtype),
            jax.ShapeDtypeStruct(state_conv.shape, F32),
            jax.ShapeDtypeStruct(state_gla.shape, F32),
        ],
        input_output_aliases={7: 0, 8: 1, 9: 2},
        compiler_params=pltpu.CompilerParams(
            dimension_semantics=("arbitrary",), vmem_limit_bytes=VMEM_LIMIT),
        name="mixer_step",
    )(proj, glog, cw, cg, gg, state_conv, state_gla, y_buf, new_conv, new_gla)


def _mixout_kernel(h_ref, y_ref, w_ref, o_ref):
    o_ref[...] = h_ref[...] + jnp.dot(y_ref[...], w_ref[...].astype(BF16), preferred_element_type=F32)


def _mixout(h, y, w_out, layer):
    n_tok = h.shape[0]
    return pl.pallas_call(
        _mixout_kernel,
        grid=(n_tok // TM_OUT,),
        in_specs=[
            pl.BlockSpec((TM_OUT, D_MODEL), lambda i: (i, 0)),
            pl.BlockSpec((TM_OUT, D_MODEL), lambda i: (i, 0)),
            pl.BlockSpec((None, D_MODEL, D_MODEL), lambda i: (layer, 0, 0), pipeline_mode=pl.Buffered(1)),
        ],
        out_specs=pl.BlockSpec((TM_OUT, D_MODEL), lambda i: (i, 0)),
        out_shape=jax.ShapeDtypeStruct((n_tok, D_MODEL), F32),
        compiler_params=pltpu.CompilerParams(
            dimension_semantics=("arbitrary",), vmem_limit_bytes=VMEM_LIMIT),
        name="mixout",
    )(h, y, w_out)


def _final_norm_kernel(h_ref, g_ref, o_ref):
    h = h_ref[...]
    o_ref[...] = h * _rms_scale(h, D_MODEL) * g_ref[...]


def _final_norm(h, gain, *, row0, n_rows, tile):
    base = row0 // tile
    assert row0 % tile == 0 and n_rows % tile == 0
    return pl.pallas_call(
        _final_norm_kernel,
        grid=(n_rows // tile,),
        in_specs=[
            pl.BlockSpec((tile, D_MODEL), lambda i: (base + i, 0)),
            pl.BlockSpec((1, D_MODEL), lambda i: (0, 0)),
        ],
        out_specs=pl.BlockSpec((tile, D_MODEL), lambda i: (i, 0)),
        out_shape=jax.ShapeDtypeStruct((n_rows, D_MODEL), F32),
        compiler_params=pltpu.CompilerParams(
            dimension_semantics=("arbitrary",), vmem_limit_bytes=VMEM_LIMIT),
        name="final_norm",
    )(h, gain)


def kernel(x_prompt, x_sample, state_conv, state_gla, meta_tokens, norm_ffn1, w_ffn1_gu, w_ffn1_down,
           norm_mix, w_mix_in, conv_w, conv_norm, gla_fgate_w2, gla_fgate_b, gla_out_norm, w_mix_out,
           norm_ffn2, w_ffn2_gu, w_ffn2_down, norm_final):
    assert x_prompt.shape == (BATCH, SEQ, D_MODEL) and x_sample.shape == (DEC_BATCH, DEC_SEQ, D_MODEL)
    h = jnp.concatenate([
        x_prompt.reshape(N_PROMPT, D_MODEL),
        x_sample.reshape(N_SAMPLE, D_MODEL),
        jnp.tile(meta_tokens.astype(x_prompt.dtype), (BATCH, 1)),
    ], axis=0)

    wgu1, wd1, wgu2, wd2, w_out = w_ffn1_gu, w_ffn1_down, w_ffn2_gu, w_ffn2_down, w_mix_out
    w_in = jnp.swapaxes(w_mix_in, 1, 2)
    fw2 = jnp.pad(gla_fgate_w2, ((0, 0), (0, LANES - GLA_GATE_RANK), (0, 0))).astype(BF16)
    n1, nm, n2 = (g.reshape(DEPTH, 1, D_MODEL) for g in (norm_ffn1, norm_mix, norm_ffn2))
    fb = gla_fgate_b.reshape(DEPTH, 1, D_QK)
    mix_w = (conv_w, conv_norm.reshape(DEPTH, 1, D_CONV), gla_out_norm.reshape(DEPTH, 1, GLA_DV))

    zero_conv = jnp.zeros((BATCH, SUBLANES, D_CONV), F32)
    zero_gla = jnp.zeros((BATCH, GLA_HEADS, GLA_DK, GLA_DV), F32)
    y = jnp.zeros((N_TOK, D_MODEL), BF16)
    conv_s = jnp.zeros(state_conv.shape, F32)
    gla_s = jnp.zeros(state_gla.shape, F32)
    conv_p, gla_p = [], []
    for l in range(DEPTH):
        h = _ffn(h, n1, wgu1, wd1, l)
        proj, glog = _mixin(h, nm, w_in, fw2, fb, l)
        y, c_m, s_m = _mixer_chain(proj, glog, *mix_w, zero_conv, zero_gla, y, l, row0=ROW_META,
                                   n_seq=BATCH, seq_len=N_META, chunk=N_META, block=N_META)
        y, c_p, s_p = _mixer_chain(proj, glog, *mix_w, c_m, s_m, y, l, row0=0,
                                   n_seq=BATCH, seq_len=SEQ, chunk=GLA_CHUNK, block=MIX_BLOCK)
        y, conv_s, gla_s = _mixer_step(proj, glog, *mix_w, state_conv, state_gla, y, conv_s, gla_s, l,
                                       row0=ROW_SAMPLE, n_seq=DEC_BATCH, seq_len=DEC_SEQ)
        h = _mixout(h, y, w_out, l)
        h = _ffn(h, n2, wgu2, wd2, l)
        conv_p.append(c_p[:, SUBLANES - (CONV_WIDTH - 1):])
        gla_p.append(s_p)

    y_prompt = _final_norm(h, norm_final[None], row0=0, n_rows=N_PROMPT, tile=512)
    y_sample = _final_norm(h, norm_final[None], row0=ROW_SAMPLE, n_rows=N_SAMPLE, tile=512)
    return (y_prompt.reshape(BATCH, SEQ, D_MODEL), y_sample.reshape(DEC_BATCH, DEC_SEQ, D_MODEL),
            jnp.stack(gla_p), jnp.stack(conv_p), gla_s, conv_s)
```

```python
import functools

import jax
import jax.numpy as jnp
from jax import lax
from jax.experimental import pallas as pl
from jax.experimental.pallas import tpu as pltpu

D_MODEL = 2048
BATCH = 4
SEQ = 2048
DEPTH = 4
DEC_BATCH = 128
DEC_SEQ = 8
N_META = 16
D_CONV = 1024
CONV_WIDTH = 3
CONV_GROUPS = 8
CONV_GROUP_WIDTH = D_CONV // CONV_GROUPS
GLA_HEADS = 4
GLA_DV = 256
GLA_DK = 128
GLA_GATE_RANK = 16
GLA_GATE_TAU = 16.0
GLA_CHUNK = 64
D_FF = 5632
EPS = 1e-6
D_QK = GLA_HEADS * GLA_DK
D_V = GLA_HEADS * GLA_DV
D_PROJ = 3 * D_CONV + 2 * D_QK + 2 * D_V

OFF_CB, OFF_CC, OFF_CH = 0, D_CONV, 2 * D_CONV
OFF_Q = 3 * D_CONV
OFF_K = OFF_Q + D_QK
OFF_V = OFF_K + D_QK
OFF_GO = OFF_V + D_V

N_PROMPT = BATCH * SEQ
N_SAMPLE = DEC_BATCH * DEC_SEQ
N_METAROWS = BATCH * N_META
ROW_SAMPLE = N_PROMPT
ROW_META = N_PROMPT + N_SAMPLE
N_TOK = ROW_META + N_METAROWS

SUBLANES = 8
LANES = 128
VMEM_LIMIT = 63 * 1024 * 1024

TM = 928
TF = 512
TN_MIX = 1536
TM_OUT = 464
SUB = 16
MAX_FACTORED_DECAY = 40.0
MIX_BLOCK = 256
SAMPLE_SEQS = 16

F32 = jnp.float32
BF16 = jnp.bfloat16


def _rms_scale(x, width):
    return lax.rsqrt(jnp.sum(x * x, axis=-1, keepdims=True) * (1.0 / width) + EPS)


def _sigmoid(x):
    return 1.0 / (1.0 + jnp.exp(-x))


def _ffn_kernel(h_ref, g_ref, wg_ref, wu_ref, wd_ref, o_ref, xn_ref):
    j = pl.program_id(1)

    @pl.when(j == 0)
    def _():
        h = h_ref[...]
        xn_ref[...] = (h * _rms_scale(h, D_MODEL) * g_ref[...]).astype(BF16)

    xn = xn_ref[...]
    a = jnp.dot(xn, wg_ref[...].astype(BF16), preferred_element_type=F32)
    b = jnp.dot(xn, wu_ref[...].astype(BF16), preferred_element_type=F32)
    act = (a * _sigmoid(a) * (b * 0.5)).astype(BF16)
    acc = jnp.where(j == 0, h_ref[...], o_ref[...])
    o_ref[...] = acc + jnp.dot(act, wd_ref[...].astype(BF16), preferred_element_type=F32)


def _ffn(h, gain, w_gu, w_down, layer):
    n_tok = h.shape[0]
    nf = D_FF // TF
    return pl.pallas_call(
        _ffn_kernel,
        grid=(n_tok // TM, nf),
        in_specs=[
            pl.BlockSpec((TM, D_MODEL), lambda i, j: (i, 0)),
            pl.BlockSpec((None, 1, D_MODEL), lambda i, j: (layer, 0, 0)),
            pl.BlockSpec((None, D_MODEL, TF), lambda i, j: (layer, 0, j)),
            pl.BlockSpec((None, D_MODEL, TF), lambda i, j: (layer, 0, j + nf)),
            pl.BlockSpec((None, TF, D_MODEL), lambda i, j: (layer, j, 0)),
        ],
        out_specs=pl.BlockSpec((TM, D_MODEL), lambda i, j: (i, 0)),
        out_shape=jax.ShapeDtypeStruct((n_tok, D_MODEL), F32),
        scratch_shapes=[pltpu.VMEM((TM, D_MODEL), BF16)],
        compiler_params=pltpu.CompilerParams(
            dimension_semantics=("arbitrary", "arbitrary"), vmem_limit_bytes=VMEM_LIMIT),
        name="ffn",
    )(h, gain, w_gu, w_gu, w_down)


_NT = (((1,), (1,)), ((), ()))


def _mixin_kernel(h_ref, g_ref, wt_ref, wflt_ref, fw2_ref, fb_ref, proj_ref, glog_ref, xn_ref):
    j = pl.program_id(1)
    half = pl.program_id(2)

    @pl.when(j == 0)
    def _():
        h = h_ref[...]
        xn = (h * _rms_scale(h, D_MODEL) * g_ref[...]).astype(BF16)
        xn_ref[half] = xn
        fl = lax.dot_general(xn, wflt_ref[...].astype(BF16), _NT, preferred_element_type=F32)
        fl = jnp.where(lax.broadcasted_iota(jnp.int32, fl.shape, 1) < GLA_GATE_RANK, fl, 0.0)
        z = jnp.dot(fl.astype(BF16), fw2_ref[...], preferred_element_type=F32) + fb_ref[...]
        glog_ref[...] = (jnp.minimum(z, 0.0) - jnp.log1p(jnp.exp(-jnp.abs(z)))) * (1.0 / GLA_GATE_TAU)

    proj_ref[...] = lax.dot_general(xn_ref[half], wt_ref[...].astype(BF16), _NT,
                                    preferred_element_type=F32).astype(proj_ref.dtype)


def _mixin(h, gain, w_in_t, fw2, fb, layer):
    n_tok = h.shape[0]
    assert n_tok % (2 * TM) == 0

    def tile(p, j, half):
        return jnp.where(j == 0, 2 * p + half, 2 * p + 1)

    return pl.pallas_call(
        _mixin_kernel,
        grid=(n_tok // (2 * TM), D_PROJ // TN_MIX, 2),
        in_specs=[
            pl.BlockSpec((TM, D_MODEL), lambda p, j, half: (tile(p, j, half), 0)),
            pl.BlockSpec((None, 1, D_MODEL), lambda p, j, half: (layer, 0, 0)),
            pl.BlockSpec((None, TN_MIX, D_MODEL), lambda p, j, half: (layer, j, 0)),
            pl.BlockSpec((None, LANES, D_MODEL), lambda p, j, half: (layer, D_PROJ // LANES, 0)),
            pl.BlockSpec((None, LANES, D_QK), lambda p, j, half: (layer, 0, 0)),
            pl.BlockSpec((None, 1, D_QK), lambda p, j, half: (layer, 0, 0)),
        ],
        out_specs=[
            pl.BlockSpec((TM, TN_MIX), lambda p, j, half: (2 * p + half, j)),
            pl.BlockSpec((TM, D_QK), lambda p, j, half: (tile(p, j, half), 0)),
        ],
        out_shape=[
            jax.ShapeDtypeStruct((n_tok, D_PROJ), BF16),
            jax.ShapeDtypeStruct((n_tok, D_QK), F32),
        ],
        scratch_shapes=[pltpu.VMEM((2, TM, D_MODEL), BF16)],
        compiler_params=pltpu.CompilerParams(
            dimension_semantics=("arbitrary", "arbitrary", "arbitrary"), vmem_limit_bytes=VMEM_LIMIT),
        name="mixin",
    )(h, gain, w_in_t, w_in_t, fw2, fb)


def _f32(proj_ref, rows, col0, width):
    return proj_ref[rows, col0:col0 + width].astype(F32)


def _block_rows(x, block, j):
    rows, width = x.shape
    if rows == block:
        return jnp.broadcast_to(x[j:j + 1, :], (rows, width))
    x3 = x.reshape(rows // block, block, width)
    return jnp.broadcast_to(x3[:, j:j + 1, :], (rows // block, block, width)).reshape(rows, width)


def _conv_branch(cb, u, u1, u2, cw_ref, cg_ref, y_ref, rows):
    conv = cw_ref[0:1, :] * u2 + cw_ref[1:2, :] * u1 + cw_ref[2:3, :] * u
    z = cb * conv
    for grp in range(CONV_GROUPS):
        cols = slice(grp * CONV_GROUP_WIDTH, (grp + 1) * CONV_GROUP_WIDTH)
        zg = z[:, cols]
        y_ref[rows, cols] = (zg * _rms_scale(zg, CONV_GROUP_WIDTH) * cg_ref[:, cols]).astype(y_ref.dtype)


def _segment_mask(rows, seg):
    ri = lax.broadcasted_iota(jnp.int32, (rows, rows), 0)
    ci = lax.broadcasted_iota(jnp.int32, (rows, rows), 1)
    causal = ci <= ri
    if rows != seg:
        causal = causal & ((ri // seg) == (ci // seg))
    return ri, ci, causal


def _cum_gates(g, seg):
    _, _, causal = _segment_mask(g.shape[0], seg)
    mcum = jnp.where(causal, 1.0, 0.0).astype(BF16)
    hi = g.astype(BF16)
    rest = g - hi.astype(F32)
    mid = rest.astype(BF16)
    lo = (rest - mid.astype(F32)).astype(BF16)
    b = jnp.dot(mcum, lo, preferred_element_type=F32)
    b = b + jnp.dot(mcum, mid, preferred_element_type=F32)
    return b + jnp.dot(mcum, hi, preferred_element_type=F32)


def _gla_rows(q, k, b, seg, sub, single_ref):
    rows = q.shape[0]
    ri, ci, causal = _segment_mask(rows, seg)
    qs = q * (GLA_DK ** -0.5)
    b_last = _block_rows(b, seg, seg - 1)
    qb = qs * jnp.exp(b)
    dec_last = jnp.exp(b_last)

    if single_ref:
        kinv = k * jnp.exp(-b)
        kdec = kinv * dec_last
        qb16, kinv16 = qb.astype(BF16), kinv.astype(BF16)
        a_heads = []
        for hd in range(GLA_HEADS):
            cols = slice(hd * GLA_DK, (hd + 1) * GLA_DK)
            a_h = lax.dot_general(qb16[:, cols], kinv16[:, cols], _NT, preferred_element_type=F32)
            a_heads.append(jnp.where(causal, a_h, 0.0))
        return qb, kdec, dec_last, a_heads

    blk0 = (ri // sub) * sub
    dcol = jnp.where(causal & (ci >= blk0), ci - blk0, -1)
    a_heads = [jnp.zeros((rows, rows), F32) for _ in range(GLA_HEADS)]
    for j in range(sub):
        kb = _block_rows(k, sub, j)
        bb = _block_rows(b, sub, j)
        p = qs * kb * jnp.exp(jnp.minimum(b - bb, 0.0))
        hit = dcol == j
        for hd in range(GLA_HEADS):
            r = jnp.sum(p[:, hd * GLA_DK:(hd + 1) * GLA_DK], axis=-1, keepdims=True)
            a_heads[hd] = jnp.where(hit, r, a_heads[hd])

    nsub = seg // sub
    if nsub > 1:
        assert rows == seg
        off = [[jnp.zeros((sub, rows), F32)] for _ in range(GLA_HEADS)]
        ci_s = lax.broadcasted_iota(jnp.int32, (sub, rows), 1)
        for i in range(1, nsub):
            lo = i * sub
            bs = b[lo - 1:lo, :]
            qd = (qs[lo:lo + sub, :] * jnp.exp(b[lo:lo + sub, :] - bs)).astype(BF16)
            kd = (k * jnp.exp(jnp.minimum(bs - b, 0.0))).astype(BF16)
            for hd in range(GLA_HEADS):
                cols = slice(hd * GLA_DK, (hd + 1) * GLA_DK)
                a_i = lax.dot_general(qd[:, cols], kd[:, cols], _NT, preferred_element_type=F32)
                off[hd].append(jnp.where(ci_s < lo, a_i, 0.0))
        a_heads = [a_heads[hd] + jnp.concatenate(off[hd], axis=0) for hd in range(GLA_HEADS)]

    kdec = k * jnp.exp(b_last - b)
    return qb, kdec, dec_last, a_heads


def _gla_out(o, go, gg_ref):
    return o * _rms_scale(o, GLA_DV) * gg_ref[...] * (go * _sigmoid(go))


def _mixer_chain_kernel(proj_ref, glog_ref, cw_ref, cg_ref, gg_ref, cprev_ref, s0_ref, y_in_ref,
                        y_ref, cnew_ref, snew_ref, s_scr, tail_scr, *, chunk, nchunks):
    del y_in_ref
    blk = pl.program_id(1)
    tb = chunk * nchunks
    sub = min(SUB, chunk)

    @pl.when(blk == 0)
    def _():
        s_scr[...] = s0_ref[0]
        tail_scr[...] = cprev_ref[0]

    u = _f32(proj_ref, slice(None), OFF_CC, D_CONV) * _f32(proj_ref, slice(None), OFF_CH, D_CONV)
    prev = tail_scr[...]
    t8 = lax.broadcasted_iota(jnp.int32, (SUBLANES, D_CONV), 0)
    r1 = pltpu.roll(u, 1, 0)
    r2 = pltpu.roll(u, 2, 0)
    f1 = jnp.where(t8 < 1, pltpu.roll(prev, 1, 0), r1[0:SUBLANES])
    f2 = jnp.where(t8 < 2, pltpu.roll(prev, 2, 0), r2[0:SUBLANES])
    if tb > SUBLANES:
        u1 = jnp.concatenate([f1, r1[SUBLANES:]], axis=0)
        u2 = jnp.concatenate([f2, r2[SUBLANES:]], axis=0)
    else:
        u1, u2 = f1, f2
    tail_scr[...] = u[tb - SUBLANES:tb]
    _conv_branch(_f32(proj_ref, slice(None), OFF_CB, D_CONV), u, u1, u2, cw_ref, cg_ref, y_ref, slice(None))

    b_blk = _cum_gates(glog_ref[...], chunk)

    def chunk_step(single_ref, c):
        rows = slice(c * chunk, (c + 1) * chunk)
        q = _f32(proj_ref, rows, OFF_Q, D_QK)
        k = _f32(proj_ref, rows, OFF_K, D_QK)
        qb, kdec, dec_last, a_heads = _gla_rows(q, k, b_blk[rows, :], chunk, sub, single_ref)
        for hd in range(GLA_HEADS):
            kc = slice(hd * GLA_DK, (hd + 1) * GLA_DK)
            v = proj_ref[rows, OFF_V + hd * GLA_DV:OFF_V + (hd + 1) * GLA_DV]
            go = _f32(proj_ref, rows, OFF_GO + hd * GLA_DV, GLA_DV)
            s = s_scr[hd]
            lhs = jnp.concatenate([qb[:, kc], a_heads[hd]], axis=1).astype(BF16)
            rhs = jnp.concatenate([s.astype(BF16), v], axis=0)
            o = jnp.dot(lhs, rhs, preferred_element_type=F32)
            y_ref[rows, D_CONV + hd * GLA_DV:D_CONV + (hd + 1) * GLA_DV] = (
                _gla_out(o, go, gg_ref).astype(y_ref.dtype))
            kdec_t = kdec[:, kc].T.astype(BF16)
            dec_col = jnp.broadcast_to(dec_last[0:1, kc], (SUBLANES, GLA_DK)).T[:, 0:1]
            s_scr[hd] = s * dec_col + jnp.dot(kdec_t, v, preferred_element_type=F32)

    def run_chunks(single_ref):
        for c in range(nchunks):
            chunk_step(single_ref, c)

    if sub == chunk:
        run_chunks(False)
    else:
        decay_bound = -chunk * jnp.min(glog_ref[...])
        lax.cond(decay_bound <= MAX_FACTORED_DECAY,
                 functools.partial(run_chunks, True), functools.partial(run_chunks, False))

    @pl.when(blk == pl.num_programs(1) - 1)
    def _():
        snew_ref[0] = s_scr[...]
        cnew_ref[0] = tail_scr[...]


def _mixer_chain(proj, glog, cw, cg, gg, cprev, s0, y_buf, layer, *, row0, n_seq, seq_len, chunk, block):
    nblk = seq_len // block
    base = row0 // block
    assert row0 % block == 0 and seq_len % block == 0 and block % chunk == 0
    kern = functools.partial(_mixer_chain_kernel, chunk=chunk, nchunks=block // chunk)
    return pl.pallas_call(
        kern,
        grid=(n_seq, nblk),
        in_specs=[
            pl.BlockSpec((block, D_PROJ), lambda s, i: (base + s * nblk + i, 0)),
            pl.BlockSpec((block, D_QK), lambda s, i: (base + s * nblk + i, 0)),
            pl.BlockSpec((None, CONV_WIDTH, D_CONV), lambda s, i: (layer, 0, 0)),
            pl.BlockSpec((None, 1, D_CONV), lambda s, i: (layer, 0, 0)),
            pl.BlockSpec((None, 1, GLA_DV), lambda s, i: (layer, 0, 0)),
            pl.BlockSpec((1, SUBLANES, D_CONV), lambda s, i: (s, 0, 0)),
            pl.BlockSpec((1, GLA_HEADS, GLA_DK, GLA_DV), lambda s, i: (s, 0, 0, 0)),
            pl.BlockSpec(memory_space=pl.ANY),
        ],
        out_specs=[
            pl.BlockSpec((block, D_MODEL), lambda s, i: (base + s * nblk + i, 0)),
            pl.BlockSpec((1, SUBLANES, D_CONV), lambda s, i: (s, 0, 0)),
            pl.BlockSpec((1, GLA_HEADS, GLA_DK, GLA_DV), lambda s, i: (s, 0, 0, 0)),
        ],
        out_shape=[
            jax.ShapeDtypeStruct(y_buf.shape, y_buf.dtype),
            jax.ShapeDtypeStruct((n_seq, SUBLANES, D_CONV), F32),
            jax.ShapeDtypeStruct((n_seq, GLA_HEADS, GLA_DK, GLA_DV), F32),
        ],
        scratch_shapes=[
            pltpu.VMEM((GLA_HEADS, GLA_DK, GLA_DV), F32),
            pltpu.VMEM((SUBLANES, D_CONV), F32),
        ],
        input_output_aliases={7: 0},
        compiler_params=pltpu.CompilerParams(
            dimension_semantics=("arbitrary", "arbitrary"), vmem_limit_bytes=VMEM_LIMIT),
        name=f"mixer_chain{chunk}",
    )(proj, glog, cw, cg, gg, cprev, s0, y_buf)


def _mixer_step_kernel(proj_ref, glog_ref, cw_ref, cg_ref, gg_ref, cprev_ref, s0_ref,
                       y_in_ref, cnew_in_ref, snew_in_ref, y_ref, cnew_ref, snew_ref, *, nseq, seq_len):
    del y_in_ref, cnew_in_ref, snew_in_ref
    rows = nseq * seq_len
    assert seq_len == SUBLANES

    u = _f32(proj_ref, slice(None), OFF_CC, D_CONV) * _f32(proj_ref, slice(None), OFF_CH, D_CONV)
    u3 = u.reshape(nseq, seq_len, D_CONV)
    prev = jnp.concatenate([jnp.zeros((nseq, seq_len - (CONV_WIDTH - 1), D_CONV), F32), cprev_ref[...]], axis=1)
    t8 = lax.broadcasted_iota(jnp.int32, (nseq, seq_len, D_CONV), 1)
    u1 = jnp.where(t8 < 1, pltpu.roll(prev, 1, 1), pltpu.roll(u3, 1, 1)).reshape(rows, D_CONV)
    u2 = jnp.where(t8 < 2, pltpu.roll(prev, 2, 1), pltpu.roll(u3, 2, 1)).reshape(rows, D_CONV)
    cnew_ref[...] = u3[:, seq_len - (CONV_WIDTH - 1):, :]
    _conv_branch(_f32(proj_ref, slice(None), OFF_CB, D_CONV), u, u1, u2, cw_ref, cg_ref, y_ref, slice(None))

    q = _f32(proj_ref, slice(None), OFF_Q, D_QK)
    k = _f32(proj_ref, slice(None), OFF_K, D_QK)
    b = _cum_gates(glog_ref[...], seq_len)
    qb, kdec, dec_last, a_heads = _gla_rows(q, k, b, seq_len, seq_len, False)

    qmask = (lax.broadcasted_iota(jnp.int32, (rows, nseq * GLA_DK), 0) // seq_len
             == lax.broadcasted_iota(jnp.int32, (rows, nseq * GLA_DK), 1) // GLA_DK)
    kmask = (lax.broadcasted_iota(jnp.int32, (nseq * GLA_DK, rows), 0) // GLA_DK
             == lax.broadcasted_iota(jnp.int32, (nseq * GLA_DK, rows), 1) // seq_len)
    for hd in range(GLA_HEADS):
        kc = slice(hd * GLA_DK, (hd + 1) * GLA_DK)
        v = proj_ref[:, OFF_V + hd * GLA_DV:OFF_V + (hd + 1) * GLA_DV]
        go = _f32(proj_ref, slice(None), OFF_GO + hd * GLA_DV, GLA_DV)
        s = s0_ref[:, hd].reshape(nseq * GLA_DK, GLA_DV)
        q_blk = jnp.where(qmask, jnp.tile(qb[:, kc], (1, nseq)), 0.0).astype(BF16)
        o = jnp.dot(q_blk, s.astype(BF16), preferred_element_type=F32)
        o = o + jnp.dot(a_heads[hd].astype(BF16), v, preferred_element_type=F32)
        y_ref[:, D_CONV + hd * GLA_DV:D_CONV + (hd + 1) * GLA_DV] = (
            _gla_out(o, go, gg_ref).astype(y_ref.dtype))
        kdec_t = kdec[:, kc].T
        k_blk = jnp.where(kmask, jnp.tile(kdec_t, (nseq, 1)), 0.0).astype(BF16)
        upd = jnp.dot(k_blk, v, preferred_element_type=F32)
        dec_t = dec_last[:, kc].T
        for n in range(nseq):
            dec_col = dec_t[:, n * seq_len:n * seq_len + 1]
            snew_ref[n, hd] = (s[n * GLA_DK:(n + 1) * GLA_DK, :] * dec_col
                               + upd[n * GLA_DK:(n + 1) * GLA_DK, :])


def _mixer_step(proj, glog, cw, cg, gg, state_conv, state_gla, y_buf, new_conv, new_gla, layer, *,
                row0, n_seq, seq_len):
    rows = SAMPLE_SEQS * seq_len
    base = row0 // rows
    assert row0 % rows == 0 and n_seq % SAMPLE_SEQS == 0
    kern = functools.partial(_mixer_step_kernel, nseq=SAMPLE_SEQS, seq_len=seq_len)
    return pl.pallas_call(
        kern,
        grid=(n_seq // SAMPLE_SEQS,),
        in_specs=[
            pl.BlockSpec((rows, D_PROJ), lambda i: (base + i, 0)),
            pl.BlockSpec((rows, D_QK), lambda i: (base + i, 0)),
            pl.BlockSpec((None, CONV_WIDTH, D_CONV), lambda i: (layer, 0, 0)),
            pl.BlockSpec((None, 1, D_CONV), lambda i: (layer, 0, 0)),
            pl.BlockSpec((None, 1, GLA_DV), lambda i: (layer, 0, 0)),
            pl.BlockSpec((None, SAMPLE_SEQS, CONV_WIDTH - 1, D_CONV), lambda i: (layer, i, 0, 0)),
            pl.BlockSpec((None, SAMPLE_SEQS, GLA_HEADS, GLA_DK, GLA_DV), lambda i: (layer, i, 0, 0, 0)),
        ] + [pl.BlockSpec(memory_space=pl.ANY)] * 3,
        out_specs=[
            pl.BlockSpec((rows, D_MODEL), lambda i: (base + i, 0)),
            pl.BlockSpec((None, SAMPLE_SEQS, CONV_WIDTH - 1, D_CONV), lambda i: (layer, i, 0, 0)),
            pl.BlockSpec((None, SAMPLE_SEQS, GLA_HEADS, GLA_DK, GLA_DV), lambda i: (layer, i, 0, 0, 0)),
        ],
        out_shape=[
            jax.ShapeDtypeStruct(y_buf.shape, y_buf.dtype),
            jax.ShapeDtypeStruct(state_conv.shape, F32),
            jax.ShapeDtypeStruct(state_gla.shape, F32),
        ],
        input_output_aliases={7: 0, 8: 1, 9: 2},
        compiler_params=pltpu.CompilerParams(
            dimension_semantics=("arbitrary",), vmem_limit_bytes=VMEM_LIMIT),
        name="mixer_step",
    )(proj, glog, cw, cg, gg, state_conv, state_gla, y_buf, new_conv, new_gla)


def _mixout_kernel(h_ref, y_ref, w_ref, o_ref):
    o_ref[...] = h_ref[...] + jnp.dot(y_ref[...], w_ref[...].astype(BF16), preferred_element_type=F32)


def _mixout(h, y, w_out, layer):
    n_tok = h.shape[0]
    return pl.pallas_call(
        _mixout_kernel,
        grid=(n_tok // TM_OUT,),
        in_specs=[
            pl.BlockSpec((TM_OUT, D_MODEL), lambda i: (i, 0)),
            pl.BlockSpec((TM_OUT, D_MODEL), lambda i: (i, 0)),
            pl.BlockSpec((None, D_MODEL, D_MODEL), lambda i: (layer, 0, 0), pipeline_mode=pl.Buffered(1)),
        ],
        out_specs=pl.BlockSpec((TM_OUT, D_MODEL), lambda i: (i, 0)),
        out_shape=jax.ShapeDtypeStruct((n_tok, D_MODEL), F32),
        compiler_params=pltpu.CompilerParams(
            dimension_semantics=("arbitrary",), vmem_limit_bytes=VMEM_LIMIT),
        name="mixout",
    )(h, y, w_out)


def _final_norm_kernel(h_ref, g_ref, o_ref):
    h = h_ref[...]
    o_ref[...] = h * _rms_scale(h, D_MODEL) * g_ref[...]


def _final_norm(h, gain, *, row0, n_rows, tile):
    base = row0 // tile
    assert row0 % tile == 0 and n_rows % tile == 0
    return pl.pallas_call(
        _final_norm_kernel,
        grid=(n_rows // tile,),
        in_specs=[
            pl.BlockSpec((tile, D_MODEL), lambda i: (base + i, 0)),
            pl.BlockSpec((1, D_MODEL), lambda i: (0, 0)),
        ],
        out_specs=pl.BlockSpec((tile, D_MODEL), lambda i: (i, 0)),
        out_shape=jax.ShapeDtypeStruct((n_rows, D_MODEL), F32),
        compiler_params=pltpu.CompilerParams(
            dimension_semantics=("arbitrary",), vmem_limit_bytes=VMEM_LIMIT),
        name="final_norm",
    )(h, gain)


def kernel(x_prompt, x_sample, state_conv, state_gla, meta_tokens, norm_ffn1, w_ffn1_gu, w_ffn1_down,
           norm_mix, w_mix_in, conv_w, conv_norm, gla_fgate_w2, gla_fgate_b, gla_out_norm, w_mix_out,
           norm_ffn2, w_ffn2_gu, w_ffn2_down, norm_final):
    assert x_prompt.shape == (BATCH, SEQ, D_MODEL) and x_sample.shape == (DEC_BATCH, DEC_SEQ, D_MODEL)
    h = jnp.concatenate([
        x_prompt.reshape(N_PROMPT, D_MODEL),
        x_sample.reshape(N_SAMPLE, D_MODEL),
        jnp.tile(meta_tokens.astype(x_prompt.dtype), (BATCH, 1)),
    ], axis=0)

    wgu1, wd1, wgu2, wd2, w_out = w_ffn1_gu, w_ffn1_down, w_ffn2_gu, w_ffn2_down, w_mix_out
    w_in = jnp.swapaxes(w_mix_in, 1, 2)
    fw2 = jnp.pad(gla_fgate_w2, ((0, 0), (0, LANES - GLA_GATE_RANK), (0, 0))).astype(BF16)
    n1, nm, n2 = (g.reshape(DEPTH, 1, D_MODEL) for g in (norm_ffn1, norm_mix, norm_ffn2))
    fb = gla_fgate_b.reshape(DEPTH, 1, D_QK)
    mix_w = (conv_w, conv_norm.reshape(DEPTH, 1, D_CONV), gla_out_norm.reshape(DEPTH, 1, GLA_DV))

    zero_conv = jnp.zeros((BATCH, SUBLANES, D_CONV), F32)
    zero_gla = jnp.zeros((BATCH, GLA_HEADS, GLA_DK, GLA_DV), F32)
    y = jnp.zeros((N_TOK, D_MODEL), BF16)
    conv_s = jnp.zeros(state_conv.shape, F32)
    gla_s = jnp.zeros(state_gla.shape, F32)
    conv_p, gla_p = [], []
    for l in range(DEPTH):
        h = _ffn(h, n1, wgu1, wd1, l)
        proj, glog = _mixin(h, nm, w_in, fw2, fb, l)
        y, c_m, s_m = _mixer_chain(proj, glog, *mix_w, zero_conv, zero_gla, y, l, row0=ROW_META,
                                   n_seq=BATCH, seq_len=N_META, chunk=N_META, block=N_META)
        y, c_p, s_p = _mixer_chain(proj, glog, *mix_w, c_m, s_m, y, l, row0=0,
                                   n_seq=BATCH, seq_len=SEQ, chunk=GLA_CHUNK, block=MIX_BLOCK)
        y, conv_s, gla_s = _mixer_step(proj, glog, *mix_w, state_conv, state_gla, y, conv_s, gla_s, l,
                                       row0=ROW_SAMPLE, n_seq=DEC_BATCH, seq_len=DEC_SEQ)
        h = _mixout(h, y, w_out, l)
        h = _ffn(h, n2, wgu2, wd2, l)
        conv_p.append(c_p[:, SUBLANES - (CONV_WIDTH - 1):])
        gla_p.append(s_p)

    y_prompt = _final_norm(h, norm_final[None], row0=0, n_rows=N_PROMPT, tile=512)
    y_sample = _final_norm(h, norm_final[None], row0=ROW_SAMPLE, n_rows=N_SAMPLE, tile=512)
    return (y_prompt.reshape(BATCH, SEQ, D_MODEL), y_sample.reshape(DEC_BATCH, DEC_SEQ, D_MODEL),
            jnp.stack(gla_p), jnp.stack(conv_p), gla_s, conv_s)
```

```python
import functools

import jax
import jax.numpy as jnp
from jax import lax
from jax.experimental import pallas as pl
from jax.experimental.pallas import tpu as pltpu

D_MODEL = 2048
BATCH = 4
SEQ = 2048
DEPTH = 4
DEC_BATCH = 128
DEC_SEQ = 8
N_META = 16
D_CONV = 1024
CONV_WIDTH = 3
CONV_GROUPS = 8
CONV_GROUP_WIDTH = D_CONV // CONV_GROUPS
GLA_HEADS = 4
GLA_DV = 256
GLA_DK = 128
GLA_GATE_RANK = 16
GLA_GATE_TAU = 16.0
GLA_CHUNK = 64
D_FF = 5632
EPS = 1e-6
D_QK = GLA_HEADS * GLA_DK
D_V = GLA_HEADS * GLA_DV
D_PROJ = 3 * D_CONV + 2 * D_QK + 2 * D_V

OFF_CB, OFF_CC, OFF_CH = 0, D_CONV, 2 * D_CONV
OFF_Q = 3 * D_CONV
OFF_K = OFF_Q + D_QK
OFF_V = OFF_K + D_QK
OFF_GO = OFF_V + D_V

N_PROMPT = BATCH * SEQ
N_SAMPLE = DEC_BATCH * DEC_SEQ
N_METAROWS = BATCH * N_META
ROW_SAMPLE = N_PROMPT
ROW_META = N_PROMPT + N_SAMPLE
N_TOK = ROW_META + N_METAROWS

SUBLANES = 8
LANES = 128
VMEM_LIMIT = 63 * 1024 * 1024

TM = 928
TF = 512
TN_MIX = 1024
TM_OUT = 464
SUB = 16
MAX_FACTORED_DECAY = 40.0
MIX_BLOCK = 256
SAMPLE_SEQS = 16

F32 = jnp.float32
BF16 = jnp.bfloat16


def _rms_scale(x, width):
    return lax.rsqrt(jnp.sum(x * x, axis=-1, keepdims=True) * (1.0 / width) + EPS)


def _sigmoid(x):
    return 1.0 / (1.0 + jnp.exp(-x))


def _ffn_kernel(h_ref, g_ref, wg_ref, wu_ref, wd_ref, o_ref, xn_ref):
    j = pl.program_id(1)

    @pl.when(j == 0)
    def _():
        h = h_ref[...]
        xn_ref[...] = (h * _rms_scale(h, D_MODEL) * g_ref[...]).astype(BF16)

    xn = xn_ref[...]
    a = jnp.dot(xn, wg_ref[...].astype(BF16), preferred_element_type=F32)
    b = jnp.dot(xn, wu_ref[...].astype(BF16), preferred_element_type=F32)
    act = (a * _sigmoid(a) * (b * 0.5)).astype(BF16)
    acc = jnp.where(j == 0, h_ref[...], o_ref[...])
    o_ref[...] = acc + jnp.dot(act, wd_ref[...].astype(BF16), preferred_element_type=F32)


def _ffn(h, gain, w_gu, w_down, layer):
    n_tok = h.shape[0]
    nf = D_FF // TF
    return pl.pallas_call(
        _ffn_kernel,
        grid=(n_tok // TM, nf),
        in_specs=[
            pl.BlockSpec((TM, D_MODEL), lambda i, j: (i, 0)),
            pl.BlockSpec((None, 1, D_MODEL), lambda i, j: (layer, 0, 0)),
            pl.BlockSpec((None, D_MODEL, TF), lambda i, j: (layer, 0, j)),
            pl.BlockSpec((None, D_MODEL, TF), lambda i, j: (layer, 0, j + nf)),
            pl.BlockSpec((None, TF, D_MODEL), lambda i, j: (layer, j, 0)),
        ],
        out_specs=pl.BlockSpec((TM, D_MODEL), lambda i, j: (i, 0)),
        out_shape=jax.ShapeDtypeStruct((n_tok, D_MODEL), F32),
        scratch_shapes=[pltpu.VMEM((TM, D_MODEL), BF16)],
        input_output_aliases={0: 0},
        compiler_params=pltpu.CompilerParams(
            dimension_semantics=("arbitrary", "arbitrary"), vmem_limit_bytes=VMEM_LIMIT),
        name="ffn",
    )(h, gain, w_gu, w_gu, w_down)


_NT = (((1,), (1,)), ((), ()))


def _mixin_kernel(h_ref, g_ref, wt_ref, wflt_ref, fw2_ref, fb_ref, proj_ref, glog_ref, xn_ref):
    j = pl.program_id(1)
    half = pl.program_id(2)

    @pl.when(j == 0)
    def _():
        h = h_ref[...]
        xn = (h * _rms_scale(h, D_MODEL) * g_ref[...]).astype(BF16)
        xn_ref[half] = xn
        fl = lax.dot_general(xn, wflt_ref[...].astype(BF16), _NT, preferred_element_type=F32)
        fl = jnp.where(lax.broadcasted_iota(jnp.int32, fl.shape, 1) < GLA_GATE_RANK, fl, 0.0)
        z = jnp.dot(fl.astype(BF16), fw2_ref[...], preferred_element_type=F32) + fb_ref[...]
        glog_ref[...] = (jnp.minimum(z, 0.0) - jnp.log1p(jnp.exp(-jnp.abs(z)))) * (1.0 / GLA_GATE_TAU)

    proj_ref[...] = lax.dot_general(xn_ref[half], wt_ref[...].astype(BF16), _NT,
                                    preferred_element_type=F32).astype(proj_ref.dtype)


def _mixin(h, gain, w_in_t, fw2, fb, layer):
    n_tok = h.shape[0]
    assert n_tok % (2 * TM) == 0

    def tile(p, j, half):
        return jnp.where(j == 0, 2 * p + half, 2 * p + 1)

    return pl.pallas_call(
        _mixin_kernel,
        grid=(n_tok // (2 * TM), D_PROJ // TN_MIX, 2),
        in_specs=[
            pl.BlockSpec((TM, D_MODEL), lambda p, j, half: (tile(p, j, half), 0)),
            pl.BlockSpec((None, 1, D_MODEL), lambda p, j, half: (layer, 0, 0)),
            pl.BlockSpec((None, TN_MIX, D_MODEL), lambda p, j, half: (layer, j, 0)),
            pl.BlockSpec((None, LANES, D_MODEL), lambda p, j, half: (layer, D_PROJ // LANES, 0)),
            pl.BlockSpec((None, LANES, D_QK), lambda p, j, half: (layer, 0, 0)),
            pl.BlockSpec((None, 1, D_QK), lambda p, j, half: (layer, 0, 0)),
        ],
        out_specs=[
            pl.BlockSpec((TM, TN_MIX), lambda p, j, half: (2 * p + half, j)),
            pl.BlockSpec((TM, D_QK), lambda p, j, half: (tile(p, j, half), 0)),
        ],
        out_shape=[
            jax.ShapeDtypeStruct((n_tok, D_PROJ), BF16),
            jax.ShapeDtypeStruct((n_tok, D_QK), F32),
        ],
        scratch_shapes=[pltpu.VMEM((2, TM, D_MODEL), BF16)],
        compiler_params=pltpu.CompilerParams(
            dimension_semantics=("arbitrary", "arbitrary", "arbitrary"), vmem_limit_bytes=VMEM_LIMIT),
        name="mixin",
    )(h, gain, w_in_t, w_in_t, fw2, fb)


def _f32(proj_ref, rows, col0, width):
    return proj_ref[rows, col0:col0 + width].astype(F32)


def _block_rows(x, block, j):
    rows, width = x.shape
    if rows == block:
        return jnp.broadcast_to(x[j:j + 1, :], (rows, width))
    x3 = x.reshape(rows // block, block, width)
    return jnp.broadcast_to(x3[:, j:j + 1, :], (rows // block, block, width)).reshape(rows, width)


def _conv_branch(cb, u, u1, u2, cw_ref, cg_ref, y_ref, rows):
    conv = cw_ref[0:1, :] * u2 + cw_ref[1:2, :] * u1 + cw_ref[2:3, :] * u
    z = cb * conv
    for grp in range(CONV_GROUPS):
        cols = slice(grp * CONV_GROUP_WIDTH, (grp + 1) * CONV_GROUP_WIDTH)
        zg = z[:, cols]
        y_ref[rows, cols] = (zg * _rms_scale(zg, CONV_GROUP_WIDTH) * cg_ref[:, cols]).astype(y_ref.dtype)


def _segment_mask(rows, seg):
    ri = lax.broadcasted_iota(jnp.int32, (rows, rows), 0)
    ci = lax.broadcasted_iota(jnp.int32, (rows, rows), 1)
    causal = ci <= ri
    if rows != seg:
        causal = causal & ((ri // seg) == (ci // seg))
    return ri, ci, causal


def _cum_gates(g, seg):
    _, _, causal = _segment_mask(g.shape[0], seg)
    mcum = jnp.where(causal, 1.0, 0.0).astype(BF16)
    hi = g.astype(BF16)
    rest = g - hi.astype(F32)
    mid = rest.astype(BF16)
    lo = (rest - mid.astype(F32)).astype(BF16)
    b = jnp.dot(mcum, lo, preferred_element_type=F32)
    b = b + jnp.dot(mcum, mid, preferred_element_type=F32)
    return b + jnp.dot(mcum, hi, preferred_element_type=F32)


def _gla_rows(q, k, b, seg, sub, single_ref):
    rows = q.shape[0]
    ri, ci, causal = _segment_mask(rows, seg)
    qs = q * (GLA_DK ** -0.5)
    b_last = _block_rows(b, seg, seg - 1)
    qb = qs * jnp.exp(b)
    dec_last = jnp.exp(b_last)

    if single_ref:
        kinv = k * jnp.exp(-b)
        kdec = kinv * dec_last
        qb16, kinv16 = qb.astype(BF16), kinv.astype(BF16)
        a_heads = []
        for hd in range(GLA_HEADS):
            cols = slice(hd * GLA_DK, (hd + 1) * GLA_DK)
            a_h = lax.dot_general(qb16[:, cols], kinv16[:, cols], _NT, preferred_element_type=F32)
            a_heads.append(jnp.where(causal, a_h, 0.0))
        return qb, kdec, dec_last, a_heads

    blk0 = (ri // sub) * sub
    dcol = jnp.where(causal & (ci >= blk0), ci - blk0, -1)
    a_heads = [jnp.zeros((rows, rows), F32) for _ in range(GLA_HEADS)]
    for j in range(sub):
        kb = _block_rows(k, sub, j)
        bb = _block_rows(b, sub, j)
        p = qs * kb * jnp.exp(jnp.minimum(b - bb, 0.0))
        hit = dcol == j
        for hd in range(GLA_HEADS):
            r = jnp.sum(p[:, hd * GLA_DK:(hd + 1) * GLA_DK], axis=-1, keepdims=True)
            a_heads[hd] = jnp.where(hit, r, a_heads[hd])

    nsub = seg // sub
    if nsub > 1:
        assert rows == seg
        off = [[jnp.zeros((sub, rows), F32)] for _ in range(GLA_HEADS)]
        ci_s = lax.broadcasted_iota(jnp.int32, (sub, rows), 1)
        for i in range(1, nsub):
            lo = i * sub
            bs = b[lo - 1:lo, :]
            qd = (qs[lo:lo + sub, :] * jnp.exp(b[lo:lo + sub, :] - bs)).astype(BF16)
            kd = (k * jnp.exp(jnp.minimum(bs - b, 0.0))).astype(BF16)
            for hd in range(GLA_HEADS):
                cols = slice(hd * GLA_DK, (hd + 1) * GLA_DK)
                a_i = lax.dot_general(qd[:, cols], kd[:, cols], _NT, preferred_element_type=F32)
                off[hd].append(jnp.where(ci_s < lo, a_i, 0.0))
        a_heads = [a_heads[hd] + jnp.concatenate(off[hd], axis=0) for hd in range(GLA_HEADS)]

    kdec = k * jnp.exp(b_last - b)
    return qb, kdec, dec_last, a_heads


def _gla_out(o, go, gg_ref):
    return o * _rms_scale(o, GLA_DV) * gg_ref[...] * (go * _sigmoid(go))


def _mixer_chain_kernel(proj_ref, glog_ref, cw_ref, cg_ref, gg_ref, cprev_ref, s0_ref, y_in_ref,
                        y_ref, cnew_ref, snew_ref, s_scr, tail_scr, *, chunk, nchunks):
    del y_in_ref
    blk = pl.program_id(1)
    tb = chunk * nchunks
    sub = min(SUB, chunk)

    @pl.when(blk == 0)
    def _():
        s_scr[...] = s0_ref[0]
        tail_scr[...] = cprev_ref[0]

    u = _f32(proj_ref, slice(None), OFF_CC, D_CONV) * _f32(proj_ref, slice(None), OFF_CH, D_CONV)
    prev = tail_scr[...]
    t8 = lax.broadcasted_iota(jnp.int32, (SUBLANES, D_CONV), 0)
    r1 = pltpu.roll(u, 1, 0)
    r2 = pltpu.roll(u, 2, 0)
    f1 = jnp.where(t8 < 1, pltpu.roll(prev, 1, 0), r1[0:SUBLANES])
    f2 = jnp.where(t8 < 2, pltpu.roll(prev, 2, 0), r2[0:SUBLANES])
    if tb > SUBLANES:
        u1 = jnp.concatenate([f1, r1[SUBLANES:]], axis=0)
        u2 = jnp.concatenate([f2, r2[SUBLANES:]], axis=0)
    else:
        u1, u2 = f1, f2
    tail_scr[...] = u[tb - SUBLANES:tb]
    _conv_branch(_f32(proj_ref, slice(None), OFF_CB, D_CONV), u, u1, u2, cw_ref, cg_ref, y_ref, slice(None))

    b_blk = _cum_gates(glog_ref[...], chunk)

    def chunk_step(single_ref, c):
        rows = slice(c * chunk, (c + 1) * chunk)
        q = _f32(proj_ref, rows, OFF_Q, D_QK)
        k = _f32(proj_ref, rows, OFF_K, D_QK)
        qb, kdec, dec_last, a_heads = _gla_rows(q, k, b_blk[rows, :], chunk, sub, single_ref)
        for hd in range(GLA_HEADS):
            kc = slice(hd * GLA_DK, (hd + 1) * GLA_DK)
            v = proj_ref[rows, OFF_V + hd * GLA_DV:OFF_V + (hd + 1) * GLA_DV]
            go = _f32(proj_ref, rows, OFF_GO + hd * GLA_DV, GLA_DV)
            s = s_scr[hd]
            lhs = jnp.concatenate([qb[:, kc], a_heads[hd]], axis=1).astype(BF16)
            rhs = jnp.concatenate([s.astype(BF16), v], axis=0)
            o = jnp.dot(lhs, rhs, preferred_element_type=F32)
            y_ref[rows, D_CONV + hd * GLA_DV:D_CONV + (hd + 1) * GLA_DV] = (
                _gla_out(o, go, gg_ref).astype(y_ref.dtype))
            kdec_t = kdec[:, kc].T.astype(BF16)
            dec_col = jnp.broadcast_to(dec_last[0:1, kc], (SUBLANES, GLA_DK)).T[:, 0:1]
            s_scr[hd] = s * dec_col + jnp.dot(kdec_t, v, preferred_element_type=F32)

    def run_chunks(single_ref):
        for c in range(nchunks):
            chunk_step(single_ref, c)

    if sub == chunk:
        run_chunks(False)
    else:
        decay_bound = -chunk * jnp.min(glog_ref[...])
        lax.cond(decay_bound <= MAX_FACTORED_DECAY,
                 functools.partial(run_chunks, True), functools.partial(run_chunks, False))

    @pl.when(blk == pl.num_programs(1) - 1)
    def _():
        snew_ref[0] = s_scr[...]
        cnew_ref[0] = tail_scr[...]


def _mixer_chain(proj, glog, cw, cg, gg, cprev, s0, y_buf, layer, *, row0, n_seq, seq_len, chunk, block):
    nblk = seq_len // block
    base = row0 // block
    assert row0 % block == 0 and seq_len % block == 0 and block % chunk == 0
    kern = functools.partial(_mixer_chain_kernel, chunk=chunk, nchunks=block // chunk)
    return pl.pallas_call(
        kern,
        grid=(n_seq, nblk),
        in_specs=[
            pl.BlockSpec((block, D_PROJ), lambda s, i: (base + s * nblk + i, 0)),
            pl.BlockSpec((block, D_QK), lambda s, i: (base + s * nblk + i, 0)),
            pl.BlockSpec((None, CONV_WIDTH, D_CONV), lambda s, i: (layer, 0, 0)),
            pl.BlockSpec((None, 1, D_CONV), lambda s, i: (layer, 0, 0)),
            pl.BlockSpec((None, 1, GLA_DV), lambda s, i: (layer, 0, 0)),
            pl.BlockSpec((1, SUBLANES, D_CONV), lambda s, i: (s, 0, 0)),
            pl.BlockSpec((1, GLA_HEADS, GLA_DK, GLA_DV), lambda s, i: (s, 0, 0, 0)),
            pl.BlockSpec(memory_space=pl.ANY),
        ],
        out_specs=[
            pl.BlockSpec((block, D_MODEL), lambda s, i: (base + s * nblk + i, 0)),
            pl.BlockSpec((1, SUBLANES, D_CONV), lambda s, i: (s, 0, 0)),
            pl.BlockSpec((1, GLA_HEADS, GLA_DK, GLA_DV), lambda s, i: (s, 0, 0, 0)),
        ],
        out_shape=[
            jax.ShapeDtypeStruct(y_buf.shape, y_buf.dtype),
            jax.ShapeDtypeStruct((n_seq, SUBLANES, D_CONV), F32),
            jax.ShapeDtypeStruct((n_seq, GLA_HEADS, GLA_DK, GLA_DV), F32),
        ],
        scratch_shapes=[
            pltpu.VMEM((GLA_HEADS, GLA_DK, GLA_DV), F32),
            pltpu.VMEM((SUBLANES, D_CONV), F32),
        ],
        input_output_aliases={7: 0},
        compiler_params=pltpu.CompilerParams(
            dimension_semantics=("arbitrary", "arbitrary"), vmem_limit_bytes=VMEM_LIMIT),
        name=f"mixer_chain{chunk}",
    )(proj, glog, cw, cg, gg, cprev, s0, y_buf)


def _mixer_step_kernel(proj_ref, glog_ref, cw_ref, cg_ref, gg_ref, cprev_ref, s0_ref,
                       y_in_ref, cnew_in_ref, snew_in_ref, y_ref, cnew_ref, snew_ref, *, nseq, seq_len):
    del y_in_ref, cnew_in_ref, snew_in_ref
    rows = nseq * seq_len
    assert seq_len == SUBLANES

    u = _f32(proj_ref, slice(None), OFF_CC, D_CONV) * _f32(proj_ref, slice(None), OFF_CH, D_CONV)
    u3 = u.reshape(nseq, seq_len, D_CONV)
    prev = jnp.concatenate([jnp.zeros((nseq, seq_len - (CONV_WIDTH - 1), D_CONV), F32), cprev_ref[...]], axis=1)
    t8 = lax.broadcasted_iota(jnp.int32, (nseq, seq_len, D_CONV), 1)
    u1 = jnp.where(t8 < 1, pltpu.roll(prev, 1, 1), pltpu.roll(u3, 1, 1)).reshape(rows, D_CONV)
    u2 = jnp.where(t8 < 2, pltpu.roll(prev, 2, 1), pltpu.roll(u3, 2, 1)).reshape(rows, D_CONV)
    cnew_ref[...] = u3[:, seq_len - (CONV_WIDTH - 1):, :]
    _conv_branch(_f32(proj_ref, slice(None), OFF_CB, D_CONV), u, u1, u2, cw_ref, cg_ref, y_ref, slice(None))

    q = _f32(proj_ref, slice(None), OFF_Q, D_QK)
    k = _f32(proj_ref, slice(None), OFF_K, D_QK)
    b = _cum_gates(glog_ref[...], seq_len)
    qb, kdec, dec_last, a_heads = _gla_rows(q, k, b, seq_len, seq_len, False)

    qmask = (lax.broadcasted_iota(jnp.int32, (rows, nseq * GLA_DK), 0) // seq_len
             == lax.broadcasted_iota(jnp.int32, (rows, nseq * GLA_DK), 1) // GLA_DK)
    kmask = (lax.broadcasted_iota(jnp.int32, (nseq * GLA_DK, rows), 0) // GLA_DK
             == lax.broadcasted_iota(jnp.int32, (nseq * GLA_DK, rows), 1) // seq_len)
    for hd in range(GLA_HEADS):
        kc = slice(hd * GLA_DK, (hd + 1) * GLA_DK)
        v = proj_ref[:, OFF_V + hd * GLA_DV:OFF_V + (hd + 1) * GLA_DV]
        go = _f32(proj_ref, slice(None), OFF_GO + hd * GLA_DV, GLA_DV)
        s = s0_ref[:, hd].reshape(nseq * GLA_DK, GLA_DV)
        q_blk = jnp.where(qmask, jnp.tile(qb[:, kc], (1, nseq)), 0.0).astype(BF16)
        o = jnp.dot(q_blk, s.astype(BF16), preferred_element_type=F32)
        o = o + jnp.dot(a_heads[hd].astype(BF16), v, preferred_element_type=F32)
        y_ref[:, D_CONV + hd * GLA_DV:D_CONV + (hd + 1) * GLA_DV] = (
            _gla_out(o, go, gg_ref).astype(y_ref.dtype))
        kdec_t = kdec[:, kc].T
        k_blk = jnp.where(kmask, jnp.tile(kdec_t, (nseq, 1)), 0.0).astype(BF16)
        upd = jnp.dot(k_blk, v, preferred_element_type=F32)
        dec_t = dec_last[:, kc].T
        for n in range(nseq):
            dec_col = dec_t[:, n * seq_len:n * seq_len + 1]
            snew_ref[n, hd] = (s[n * GLA_DK:(n + 1) * GLA_DK, :] * dec_col
                               + upd[n * GLA_DK:(n + 1) * GLA_DK, :])


def _mixer_step(proj, glog, cw, cg, gg, state_conv, state_gla, y_buf, new_conv, new_gla, layer, *,
                row0, n_seq, seq_len):
    rows = SAMPLE_SEQS * seq_len
    base = row0 // rows
    assert row0 % rows == 0 and n_seq % SAMPLE_SEQS == 0
    kern = functools.partial(_mixer_step_kernel, nseq=SAMPLE_SEQS, seq_len=seq_len)
    return pl.pallas_call(
        kern,
        grid=(n_seq // SAMPLE_SEQS,),
        in_specs=[
            pl.BlockSpec((rows, D_PROJ), lambda i: (base + i, 0)),
            pl.BlockSpec((rows, D_QK), lambda i: (base + i, 0)),
            pl.BlockSpec((None, CONV_WIDTH, D_CONV), lambda i: (layer, 0, 0)),
            pl.BlockSpec((None, 1, D_CONV), lambda i: (layer, 0, 0)),
            pl.BlockSpec((None, 1, GLA_DV), lambda i: (layer, 0, 0)),
            pl.BlockSpec((None, SAMPLE_SEQS, CONV_WIDTH - 1, D_CONV), lambda i: (layer, i, 0, 0)),
            pl.BlockSpec((None, SAMPLE_SEQS, GLA_HEADS, GLA_DK, GLA_DV), lambda i: (layer, i, 0, 0, 0)),
        ] + [pl.BlockSpec(memory_space=pl.ANY)] * 3,
        out_specs=[
            pl.BlockSpec((rows, D_MODEL), lambda i: (base + i, 0)),
            pl.BlockSpec((None, SAMPLE_SEQS, CONV_WIDTH - 1, D_CONV), lambda i: (layer, i, 0, 0)),
            pl.BlockSpec((None, SAMPLE_SEQS, GLA_HEADS, GLA_DK, GLA_DV), lambda i: (layer, i, 0, 0, 0)),
        ],
        out_shape=[
            jax.ShapeDtypeStruct(y_buf.shape, y_buf.dtype),
            jax.ShapeDtypeStruct(state_conv.shape, F32),
            jax.ShapeDtypeStruct(state_gla.shape, F32),
        ],
        input_output_aliases={7: 0, 8: 1, 9: 2},
        compiler_params=pltpu.CompilerParams(
            dimension_semantics=("arbitrary",), vmem_limit_bytes=VMEM_LIMIT),
        name="mixer_step",
    )(proj, glog, cw, cg, gg, state_conv, state_gla, y_buf, new_conv, new_gla)


def _mixout_kernel(h_ref, y_ref, w_ref, o_ref):
    o_ref[...] = h_ref[...] + jnp.dot(y_ref[...], w_ref[...].astype(BF16), preferred_element_type=F32)


def _mixout(h, y, w_out, layer):
    n_tok = h.shape[0]
    return pl.pallas_call(
        _mixout_kernel,
        grid=(n_tok // TM_OUT,),
        in_specs=[
            pl.BlockSpec((TM_OUT, D_MODEL), lambda i: (i, 0)),
            pl.BlockSpec((TM_OUT, D_MODEL), lambda i: (i, 0)),
            pl.BlockSpec((None, D_MODEL, D_MODEL), lambda i: (layer, 0, 0), pipeline_mode=pl.Buffered(1)),
        ],
        out_specs=pl.BlockSpec((TM_OUT, D_MODEL), lambda i: (i, 0)),
        out_shape=jax.ShapeDtypeStruct((n_tok, D_MODEL), F32),
        input_output_aliases={0: 0},
        compiler_params=pltpu.CompilerParams(
            dimension_semantics=("arbitrary",), vmem_limit_bytes=VMEM_LIMIT),
        name="mixout",
    )(h, y, w_out)


def _final_norm_kernel(h_ref, g_ref, o_ref):
    h = h_ref[...]
    o_ref[...] = h * _rms_scale(h, D_MODEL) * g_ref[...]


def _final_norm(h, gain, *, row0, n_rows, tile):
    base = row0 // tile
    assert row0 % tile == 0 and n_rows % tile == 0
    return pl.pallas_call(
        _final_norm_kernel,
        grid=(n_rows // tile,),
        in_specs=[
            pl.BlockSpec((tile, D_MODEL), lambda i: (base + i, 0)),
            pl.BlockSpec((1, D_MODEL), lambda i: (0, 0)),
        ],
        out_specs=pl.BlockSpec((tile, D_MODEL), lambda i: (i, 0)),
        out_shape=jax.ShapeDtypeStruct((n_rows, D_MODEL), F32),
        compiler_params=pltpu.CompilerParams(
            dimension_semantics=("arbitrary",), vmem_limit_bytes=VMEM_LIMIT),
        name="final_norm",
    )(h, gain)


def kernel(x_prompt, x_sample, state_conv, state_gla, meta_tokens, norm_ffn1, w_ffn1_gu, w_ffn1_down,
           norm_mix, w_mix_in, conv_w, conv_norm, gla_fgate_w2, gla_fgate_b, gla_out_norm, w_mix_out,
           norm_ffn2, w_ffn2_gu, w_ffn2_down, norm_final):
    assert x_prompt.shape == (BATCH, SEQ, D_MODEL) and x_sample.shape == (DEC_BATCH, DEC_SEQ, D_MODEL)
    h = jnp.concatenate([
        x_prompt.reshape(N_PROMPT, D_MODEL),
        x_sample.reshape(N_SAMPLE, D_MODEL),
        jnp.tile(meta_tokens.astype(x_prompt.dtype), (BATCH, 1)),
    ], axis=0)

    wgu1, wd1, wgu2, wd2, w_out = w_ffn1_gu, w_ffn1_down, w_ffn2_gu, w_ffn2_down, w_mix_out
    w_in = jnp.swapaxes(w_mix_in, 1, 2)
    fw2 = jnp.pad(gla_fgate_w2, ((0, 0), (0, LANES - GLA_GATE_RANK), (0, 0))).astype(BF16)
    n1, nm, n2 = (g.reshape(DEPTH, 1, D_MODEL) for g in (norm_ffn1, norm_mix, norm_ffn2))
    fb = gla_fgate_b.reshape(DEPTH, 1, D_QK)
    mix_w = (conv_w, conv_norm.reshape(DEPTH, 1, D_CONV), gla_out_norm.reshape(DEPTH, 1, GLA_DV))

    zero_conv = jnp.zeros((BATCH, SUBLANES, D_CONV), F32)
    zero_gla = jnp.zeros((BATCH, GLA_HEADS, GLA_DK, GLA_DV), F32)
    y = jnp.zeros((N_TOK, D_MODEL), BF16)
    conv_s = jnp.zeros(state_conv.shape, F32)
    gla_s = jnp.zeros(state_gla.shape, F32)
    conv_p, gla_p = [], []
    for l in range(DEPTH):
        h = _ffn(h, n1, wgu1, wd1, l)
        proj, glog = _mixin(h, nm, w_in, fw2, fb, l)
        y, c_m, s_m = _mixer_chain(proj, glog, *mix_w, zero_conv, zero_gla, y, l, row0=ROW_META,
                                   n_seq=BATCH, seq_len=N_META, chunk=N_META, block=N_META)
        y, c_p, s_p = _mixer_chain(proj, glog, *mix_w, c_m, s_m, y, l, row0=0,
                                   n_seq=BATCH, seq_len=SEQ, chunk=GLA_CHUNK, block=MIX_BLOCK)
        y, conv_s, gla_s = _mixer_step(proj, glog, *mix_w, state_conv, state_gla, y, conv_s, gla_s, l,
                                       row0=ROW_SAMPLE, n_seq=DEC_BATCH, seq_len=DEC_SEQ)
        h = _mixout(h, y, w_out, l)
        h = _ffn(h, n2, wgu2, wd2, l)
        conv_p.append(c_p[:, SUBLANES - (CONV_WIDTH - 1):])
        gla_p.append(s_p)

    y_prompt = _final_norm(h, norm_final[None], row0=0, n_rows=N_PROMPT, tile=512)
    y_sample = _final_norm(h, norm_final[None], row0=ROW_SAMPLE, n_rows=N_SAMPLE, tile=512)
    return (y_prompt.reshape(BATCH, SEQ, D_MODEL), y_sample.reshape(DEC_BATCH, DEC_SEQ, D_MODEL),
            jnp.stack(gla_p), jnp.stack(conv_p), gla_s, conv_s)
```
